```python
import jax, jax.numpy as jnp
from jax import lax
import numpy as np

D_MODEL = 2048
BATCH = 1
SEQ = 8192
DEPTH = 4
DEC_BATCH = 16
DEC_SEQ = 32
PAST_LEN = 4096

CHUNK = 64
FOX_HEAD_DIM = 64
FOX_WIDTH = D_MODEL // 2
FOX_HEADS = FOX_WIDTH // FOX_HEAD_DIM
HGRN_HEAD_DIM = 128
HGRN_WIDTH = D_MODEL // 4
HGRN_HEADS = HGRN_WIDTH // HGRN_HEAD_DIM
CONV_CH = D_MODEL // 4
CONV_WIDTH = 31
MIX_WIDTH = FOX_WIDTH + HGRN_WIDTH + CONV_CH
N_IN = 3 * FOX_WIDTH + FOX_HEADS + 4 * HGRN_WIDTH + 2 * CONV_CH
D_FF = 11 * D_MODEL // 4
Q_BLOCK = 128
N_SUB = 3
EPS = 1e-6
NEG = -1e30

kernel_name = "hybrid_fox_hgrn2_conformer_stream_step"


def rms_norm(x, g):
    xf = x.astype(jnp.float32)
    y = xf * lax.rsqrt(jnp.mean(xf * xf, axis=-1, keepdims=True) + EPS)
    return (y * g.astype(jnp.float32)).astype(x.dtype)


def layer_norm(x, g, b):
    xf = x.astype(jnp.float32)
    mu = jnp.mean(xf, axis=-1, keepdims=True)
    var = jnp.mean(jnp.square(xf - mu), axis=-1, keepdims=True)
    y = (xf - mu) * lax.rsqrt(var + EPS) * g.astype(jnp.float32) + b.astype(jnp.float32)
    return y.astype(x.dtype)


def modulate(x, g, shift, scale):
    return rms_norm(x, g) * (1 + scale[:, None, :]) + shift[:, None, :]


def swiglu_ffn(h, w_in, w_out):
    a, b = jnp.split(h @ w_in, 2, axis=-1)
    return (jax.nn.silu(a) * b) @ w_out


def fox_attention(q, k, v, fq, fk, q_pos, k_pos):
    B, T, H, Dh = q.shape
    qb = min(Q_BLOCK, T)
    nb = T // qb
    scale = Dh ** -0.5
    qs = q.reshape(B, nb, qb, H, Dh).transpose(1, 0, 2, 3, 4)
    fqs = fq.reshape(B, nb, qb, H).transpose(1, 0, 2, 3)
    ps = q_pos.reshape(nb, qb)
    fk_t = fk.transpose(0, 2, 1)

    def block(args):
        qi, fi, pi = args
        s = jnp.einsum('bqhd,bkhd->bhqk', qi, k, preferred_element_type=jnp.float32) * scale
        s = s + fi.transpose(0, 2, 1)[..., None] - fk_t[:, :, None, :]
        mask = k_pos[None, :] <= pi[:, None]
        s = jnp.where(mask[None, None], s, NEG)
        p = jax.nn.softmax(s, axis=-1)
        return jnp.einsum('bhqk,bkhd->bqhd', p.astype(v.dtype), v)

    o = lax.map(block, (qs, fqs, ps))
    return o.transpose(1, 0, 2, 3, 4).reshape(B, T, H, Dh)


def hgrn_recurrence(q, k, v, logf, s0):
    B, T, H, K = q.shape
    V = v.shape[-1]
    L = min(CHUNK, T)
    n = T // L

    def blocks(a):
        return a.astype(jnp.float32).reshape(B, n, L, H, a.shape[-1]).transpose(1, 0, 2, 3, 4)

    causal = jnp.tril(jnp.ones((L, L), dtype=bool))[None, :, :, None, None]

    def step(S, inp):
        qc, kc, vc, gc = inp
        b = jnp.cumsum(gc, axis=1)
        o_inter = jnp.einsum('bthk,bhkv->bthv', qc * jnp.exp(b), S)
        diff = b[:, :, None] - b[:, None, :]
        decay = jnp.where(causal, jnp.exp(jnp.where(causal, diff, 0.0)), 0.0)
        att = jnp.einsum('bthk,btshk->bhts', qc, decay * kc[:, None])
        o_intra = jnp.einsum('bhts,bshv->bthv', att, vc)
        b_last = b[:, -1]
        S_new = jnp.exp(b_last)[..., None] * S + jnp.einsum(
            'bshk,bshv->bhkv', kc * jnp.exp(b_last[:, None] - b), vc)
        return S_new, o_inter + o_intra

    s_fin, o = lax.scan(step, s0.astype(jnp.float32), (blocks(q), blocks(k), blocks(v), blocks(logf)))
    return o.transpose(1, 0, 2, 3, 4).reshape(B, T, H, V), s_fin


def causal_dwconv(u, buf, w, bias):
    xp = jnp.concatenate([buf.astype(u.dtype), u], axis=1)
    y = lax.conv_general_dilated(
        xp, w[:, None, :].astype(u.dtype), window_strides=(1,), padding='VALID',
        dimension_numbers=('NWC', 'WIO', 'NWC'), feature_group_count=u.shape[-1])
    return y + bias, xp[:, -(CONV_WIDTH - 1):]


def token_mixer(h, w_in, w_out, fox_bias, lb, hgrn_g, dw_w, dw_b, ln_g, ln_b, hist):
    B, T, _ = h.shape
    sizes = [FOX_WIDTH] * 3 + [FOX_HEADS] + [HGRN_WIDTH] * 4 + [CONV_CH] * 2
    idx = list(np.cumsum(sizes)[:-1])
    q, k, v, fz, hq, hf, hi, hg, ga, gb = jnp.split(h @ w_in, idx, axis=-1)
    q = q.reshape(B, T, FOX_HEADS, FOX_HEAD_DIM)
    k = k.reshape(B, T, FOX_HEADS, FOX_HEAD_DIM)
    v = v.reshape(B, T, FOX_HEADS, FOX_HEAD_DIM)
    logf = jax.nn.log_sigmoid(fz.astype(jnp.float32) + fox_bias.astype(jnp.float32))
    if hist is None:
        P = 0
        k_all, v_all, logf_all = k, v, logf
        s0 = jnp.zeros((B, HGRN_HEADS, HGRN_HEAD_DIM, HGRN_HEAD_DIM), jnp.float32)
        buf = jnp.zeros((B, CONV_WIDTH - 1, CONV_CH), h.dtype)
    else:
        ck, cv, clogf, s0, buf = hist
        P = ck.shape[1]
        k_all = jnp.concatenate([ck.astype(k.dtype), k], axis=1)
        v_all = jnp.concatenate([cv.astype(v.dtype), v], axis=1)
        logf_all = jnp.concatenate([clogf.astype(jnp.float32), logf], axis=1)
    F = jnp.cumsum(logf_all, axis=1)
    q_pos = P + jnp.arange(T)
    k_pos = jnp.arange(P + T)
    fox = fox_attention(q, k_all, v_all, F[:, P:], F, q_pos, k_pos).reshape(B, T, FOX_WIDTH)
    shp = (B, T, HGRN_HEADS, HGRN_HEAD_DIM)
    lbh = lb.reshape(HGRN_HEADS, HGRN_HEAD_DIM)
    f = lbh + (1.0 - lbh) * jax.nn.sigmoid(hf.astype(jnp.float32).reshape(shp))
    o, s_fin = hgrn_recurrence(hq.reshape(shp), 1.0 - f, hi.reshape(shp), jnp.log(f), s0)
    hgrn = (rms_norm(o, hgrn_g) * jax.nn.silu(hg.reshape(shp).astype(jnp.float32)))
    hgrn = hgrn.reshape(B, T, HGRN_WIDTH).astype(h.dtype)
    u = ga * jax.nn.sigmoid(gb)
    y, new_buf = causal_dwconv(u, buf, dw_w, dw_b)
    conv = jax.nn.silu(layer_norm(y, ln_g, ln_b))
    out = jnp.concatenate([fox, hgrn, conv], axis=-1) @ w_out
    new_state = (k, v, logf.astype(h.dtype), s_fin.astype(h.dtype), new_buf)
    return out, new_state


def setup_inputs(seed: int = 0) -> dict:
    key = jax.random.key(seed)
    ks = jax.random.split(key, 26)
    nrm = lambda i, shape: jax.random.normal(ks[i], shape, jnp.float32)
    D = D_MODEL
    return {
        "x_prompt": nrm(0, (BATCH, SEQ, D)),
        "x_sample": nrm(1, (DEC_BATCH, DEC_SEQ, D)),
        "cache_fox_k": nrm(2, (DEPTH, DEC_BATCH, PAST_LEN, FOX_HEADS, FOX_HEAD_DIM)),
        "cache_fox_v": nrm(3, (DEPTH, DEC_BATCH, PAST_LEN, FOX_HEADS, FOX_HEAD_DIM)),
        "cache_fox_logf": jax.nn.log_sigmoid(2.0 + nrm(4, (DEPTH, DEC_BATCH, PAST_LEN, FOX_HEADS))),
        "state_hgrn": 0.5 * nrm(5, (DEPTH, DEC_BATCH, HGRN_HEADS, HGRN_HEAD_DIM, HGRN_HEAD_DIM)),
        "state_conv": nrm(6, (DEPTH, DEC_BATCH, CONV_WIDTH - 1, CONV_CH)),
        "c_prompt": nrm(7, (BATCH, D)),
        "c_sample": nrm(8, (DEC_BATCH, D)),
        "norm_g": 1.0 + 0.02 * nrm(9, (DEPTH, N_SUB, D)),
        "ada_w": 0.5 * D ** -0.5 * nrm(10, (DEPTH, D, N_SUB * 3 * D)),
        "ada_b": 0.02 * nrm(11, (DEPTH, N_SUB * 3 * D)),
        "ffn_w_in": D ** -0.5 * nrm(12, (DEPTH, 2, D, 2 * D_FF)),
        "ffn_w_out": D_FF ** -0.5 * nrm(13, (DEPTH, 2, D_FF, D)),
        "mix_w_in": D ** -0.5 * nrm(14, (DEPTH, D, N_IN)),
        "mix_w_out": MIX_WIDTH ** -0.5 * nrm(15, (DEPTH, MIX_WIDTH, D)),
        "fox_f_bias": 2.0 + 0.5 * nrm(16, (DEPTH, FOX_HEADS)),
        "hgrn_lb_logits": nrm(17, (DEPTH, HGRN_WIDTH)),
        "hgrn_norm_g": 1.0 + 0.02 * nrm(18, (DEPTH, HGRN_HEAD_DIM)),
        "conv_dw_w": CONV_WIDTH ** -0.5 * nrm(19, (DEPTH, CONV_WIDTH, CONV_CH)),
        "conv_dw_b": 0.02 * nrm(20, (DEPTH, CONV_CH)),
        "conv_ln_g": 1.0 + 0.02 * nrm(21, (DEPTH, CONV_CH)),
        "conv_ln_b": 0.02 * nrm(22, (DEPTH, CONV_CH)),
        "final_norm_g": 1.0 + 0.02 * nrm(23, (D,)),
    }


def reference(x_prompt, x_sample, cache_fox_k, cache_fox_v, cache_fox_logf, state_hgrn, state_conv,
              c_prompt, c_sample, norm_g, ada_w, ada_b, ffn_w_in, ffn_w_out, mix_w_in, mix_w_out,
              fox_f_bias, hgrn_lb_logits, hgrn_norm_g, conv_dw_w, conv_dw_b, conv_ln_g, conv_ln_b,
              final_norm_g):
    lb_sm = jnp.cumsum(jax.nn.softmax(hgrn_lb_logits.astype(jnp.float32), axis=0), axis=0)
    lb_all = lb_sm - lb_sm[0:1]

    def trunk(x, c, hist):
        B = x.shape[0]
        c_act = jax.nn.silu(c)
        new = []
        for l in range(DEPTH):
            mod = (c_act @ ada_w[l] + ada_b[l]).reshape(B, N_SUB, 3, D_MODEL)
            h = modulate(x, norm_g[l, 0], mod[:, 0, 0], mod[:, 0, 1])
            x = x + 0.5 * mod[:, 0, 2][:, None, :] * swiglu_ffn(h, ffn_w_in[l, 0], ffn_w_out[l, 0])
            h = modulate(x, norm_g[l, 1], mod[:, 1, 0], mod[:, 1, 1])
            layer_hist = None if hist is None else tuple(a[l] for a in hist)
            m, st = token_mixer(h, mix_w_in[l], mix_w_out[l], fox_f_bias[l], lb_all[l], hgrn_norm_g[l],
                                conv_dw_w[l], conv_dw_b[l], conv_ln_g[l], conv_ln_b[l], layer_hist)
            x = x + mod[:, 1, 2][:, None, :] * m
            h = modulate(x, norm_g[l, 2], mod[:, 2, 0], mod[:, 2, 1])
            x = x + 0.5 * mod[:, 2, 2][:, None, :] * swiglu_ffn(h, ffn_w_in[l, 1], ffn_w_out[l, 1])
            new.append(st)
        y = rms_norm(x, final_norm_g)
        stacked = [jnp.stack([s[i] for s in new]) for i in range(5)]
        return y, stacked

    y_prompt, (kp, vp, fp, hp, cp) = trunk(x_prompt, c_prompt, None)
    y_sample, (ks, vs, fs, hs, cs) = trunk(
        x_sample, c_sample, (cache_fox_k, cache_fox_v, cache_fox_logf, state_hgrn, state_conv))
    return (y_prompt, y_sample, kp, vp, fp, hp, cp, ks, vs, fs, hs, cs)
```

```python
import functools

import jax
import jax.numpy as jnp
import numpy as np
from jax import lax
from jax.experimental import pallas as pl
from jax.experimental.pallas import tpu as pltpu

F32 = jnp.float32
BF16 = jnp.bfloat16
EPS = 1e-6
NEG = -1e30
LANE = 128
HGRN_SUB = 16
VMEM_LIMIT = 60 * 1024 * 1024


def _pick(n, target, mult=8):
    for t in range(min(n, target), 0, -1):
        if n % t == 0 and t % mult == 0:
            return t
    return n


def _cparams(sem):
    return pltpu.CompilerParams(dimension_semantics=sem, vmem_limit_bytes=VMEM_LIMIT)


def _sigmoid(x):
    return 1.0 / (1.0 + jnp.exp(-x))


def _modulate(x, g, shift, scale):
    tm, d = x.shape
    grp = shift.shape[0]
    y = x * lax.rsqrt(jnp.mean(x * x, axis=-1, keepdims=True) + EPS) * g
    y = y.reshape(grp, tm // grp, d) * (1.0 + scale) + shift
    return y.reshape(tm, d)


def _gated_residual(x, gate, upd, coef):
    tm, d = x.shape
    grp = gate.shape[0]
    y = x.reshape(grp, tm // grp, d) + (coef * gate) * upd.reshape(grp, tm // grp, d)
    return y.reshape(tm, d)


def _mod_spec(tm, seq_len, d):
    grp = max(1, tm // seq_len)
    return pl.BlockSpec((grp, 1, d), lambda i, j: ((i * tm) // (seq_len * grp), 0, 0))


def _ada_kernel(c_ref, w_ref, b_ref, o_ref):
    w = w_ref[...].astype(BF16)
    o_ref[...] = jnp.dot(c_ref[...], w, preferred_element_type=F32) + b_ref[...]


def _ada(c_act, ada_w, ada_b):
    depth, d, n = ada_w.shape
    r = c_act.shape[0]
    tn = _pick(n, 1024, LANE)
    return pl.pallas_call(
        _ada_kernel,
        grid=(depth, n // tn),
        in_specs=[pl.BlockSpec((r, d), lambda l, j: (0, 0)),
                  pl.BlockSpec((None, d, tn), lambda l, j: (l, 0, j)),
                  pl.BlockSpec((None, 1, tn), lambda l, j: (l, 0, j))],
        out_specs=pl.BlockSpec((None, r, tn), lambda l, j: (l, 0, j)),
        out_shape=jax.ShapeDtypeStruct((depth, r, n), F32),
        compiler_params=_cparams(("arbitrary", "arbitrary")),
        name="ada_proj",
    )(c_act, ada_w, ada_b.reshape(depth, 1, n))


def _ffn_kernel(x_ref, g_ref, sh_ref, sc_ref, gt_ref, wa_ref, wb_ref, wo_ref, o_ref, h_ref, *, nj):
    j = pl.program_id(1)

    @pl.when(j == 0)
    def _():
        h_ref[...] = _modulate(x_ref[...], g_ref[...], sh_ref[...], sc_ref[...]).astype(BF16)

    h = h_ref[...]
    a = jnp.dot(h, wa_ref[...], preferred_element_type=F32)
    b = jnp.dot(h, wb_ref[...], preferred_element_type=F32)
    act = (a * _sigmoid(a) * b).astype(BF16)
    part = jnp.dot(act, wo_ref[...], preferred_element_type=F32)

    @pl.when(j == 0)
    def _():
        o_ref[...] = part

    @pl.when(j > 0)
    def _():
        o_ref[...] += part

    @pl.when(j == nj - 1)
    def _():
        o_ref[...] = _gated_residual(x_ref[...], gt_ref[...], o_ref[...], 0.5)


def _ffn(x, g, shift, scale, gate, w_in, w_out, l, s, seq_len, tm):
    m, d = x.shape
    dff = w_out.shape[2]
    tf = _pick(dff, 256, LANE)
    nj = dff // tf
    mod = _mod_spec(tm, seq_len, d)
    return pl.pallas_call(
        functools.partial(_ffn_kernel, nj=nj),
        grid=(m // tm, nj),
        in_specs=[pl.BlockSpec((tm, d), lambda i, j: (i, 0)),
                  pl.BlockSpec((1, d), lambda i, j: (0, 0)),
                  mod, mod, mod,
                  pl.BlockSpec((None, None, d, tf), lambda i, j: (l, s, 0, j)),
                  pl.BlockSpec((None, None, d, tf), lambda i, j: (l, s, 0, j + nj)),
                  pl.BlockSpec((None, None, tf, d), lambda i, j: (l, s, j, 0))],
        out_specs=pl.BlockSpec((tm, d), lambda i, j: (i, 0)),
        out_shape=jax.ShapeDtypeStruct((m, d), F32),
        scratch_shapes=[pltpu.VMEM((tm, d), BF16)],
        compiler_params=_cparams(("arbitrary", "arbitrary")),
        name="ffn",
    )(x, g, shift, scale, gate, w_in, w_in, w_out)


def _inproj_kernel(x_ref, g_ref, sh_ref, sc_ref, w_ref, proj_ref, qkv_ref, h_ref, *, nq, nqkv, qscale):
    j = pl.program_id(1)

    @pl.when(j == 0)
    def _():
        h_ref[...] = _modulate(x_ref[...], g_ref[...], sh_ref[...], sc_ref[...]).astype(BF16)

    r = jnp.dot(h_ref[...], w_ref[...], preferred_element_type=F32)
    proj_ref[...] = r

    @pl.when(j < nqkv)
    def _():
        qkv_ref[...] = (r * jnp.where(j < nq, qscale, 1.0)).astype(BF16)


def _inproj(x, g, shift, scale, w_r, l, fw, tn, qscale, seq_len, tm):
    m, d = x.shape
    n_pad = w_r.shape[2]
    nq, nqkv = fw // tn, 3 * fw // tn
    mod = _mod_spec(tm, seq_len, d)
    return pl.pallas_call(
        functools.partial(_inproj_kernel, nq=nq, nqkv=nqkv, qscale=qscale),
        grid=(m // tm, n_pad // tn),
        in_specs=[pl.BlockSpec((tm, d), lambda i, j: (i, 0)),
                  pl.BlockSpec((1, d), lambda i, j: (0, 0)),
                  mod, mod,
                  pl.BlockSpec((None, d, tn), lambda i, j: (l, 0, j))],
        out_specs=[pl.BlockSpec((tm, tn), lambda i, j: (i, j)),
                   pl.BlockSpec((tm, tn), lambda i, j: (i, jnp.minimum(j, nqkv - 1)))],
        out_shape=[jax.ShapeDtypeStruct((m, n_pad), F32),
                   jax.ShapeDtypeStruct((m, 3 * fw), BF16)],
        scratch_shapes=[pltpu.VMEM((tm, d), BF16)],
        compiler_params=_cparams(("arbitrary", "arbitrary")),
        name="mix_inproj",
    )(x, g, shift, scale, w_r)


def _cumsum_kernel(z_ref, bias_ref, lf_ref, f_ref, carry_ref, *, logsig):
    t = pl.program_id(1)

    @pl.when(t == 0)
    def _():
        carry_ref[...] = jnp.zeros_like(carry_ref)

    z = z_ref[...]
    if logsig:
        z = z + bias_ref[...]
        z = jnp.minimum(z, 0.0) - jnp.log(1.0 + jnp.exp(-jnp.abs(z)))
    lf_ref[...] = z
    tb = z.shape[0]
    tri = (lax.broadcasted_iota(jnp.int32, (tb, tb), 0)
           >= lax.broadcasted_iota(jnp.int32, (tb, tb), 1)).astype(BF16)
    hi = z.astype(BF16)
    r1 = z - hi.astype(F32)
    mid = r1.astype(BF16)
    lo = (r1 - mid.astype(F32)).astype(BF16)
    cs = (jnp.dot(tri, hi, preferred_element_type=F32) + jnp.dot(tri, mid, preferred_element_type=F32)
          + jnp.dot(tri, lo, preferred_element_type=F32))
    f = cs + carry_ref[...]
    f_ref[...] = f
    carry_ref[...] = f[tb - 1:tb, :]


def _logf_cumsum(z, bias, logsig):
    b, t, h = z.shape
    tb = _pick(t, 512, 8)
    spec = pl.BlockSpec((None, tb, h), lambda i, j: (i, j, 0))
    return pl.pallas_call(
        functools.partial(_cumsum_kernel, logsig=logsig),
        grid=(b, t // tb),
        in_specs=[spec, pl.BlockSpec((1, h), lambda i, j: (0, 0))],
        out_specs=[spec, spec],
        out_shape=[jax.ShapeDtypeStruct((b, t, h), F32)] * 2,
        scratch_shapes=[pltpu.VMEM((1, h), F32)],
        compiler_params=_cparams(("arbitrary", "arbitrary")),
        name="logf_cumsum",
    )(z, bias)


def _head_masks(hd):
    lane = lax.broadcasted_iota(jnp.int32, (1, LANE), 1)
    return [(lane >= hd * h) & (lane < hd * (h + 1)) for h in range(LANE // hd)]


def _attn_kernel(qt_ref, kt_ref, q_ref, k_ref, v_ref, fq_ref, fk_ref, o_ref, m_ref, l_ref, acc_ref,
                 *, tq, tk, hd):
    t = pl.program_id(2)
    qi, ki = qt_ref[t], kt_ref[t]
    masks = _head_masks(hd)

    @pl.when(ki == 0)
    def _():
        m_ref[...] = jnp.full_like(m_ref, NEG)
        l_ref[...] = jnp.zeros_like(l_ref)
        acc_ref[...] = jnp.zeros_like(acc_ref)

    def body(masked):
        q, k, v = q_ref[...], k_ref[...], v_ref[...]
        if masked:
            row = qi * tq + lax.broadcasted_iota(jnp.int32, (tq, tk), 0)
            col = ki * tk + lax.broadcasted_iota(jnp.int32, (tq, tk), 1)
            visible = col <= row
        acc = acc_ref[...]
        for h, hm in enumerate(masks):
            s = lax.dot_general(jnp.where(hm, q, 0), k, (((1,), (1,)), ((), ())),
                                preferred_element_type=F32)
            s = s + fq_ref[:, h:h + 1] - fk_ref[h:h + 1, :]
            if masked:
                s = jnp.where(visible, s, NEG)
            m_prev = m_ref[h]
            m_new = jnp.maximum(m_prev, jnp.max(s, axis=-1, keepdims=True))
            alpha = jnp.exp(m_prev - m_new)
            p = jnp.exp(s - m_new)
            l_ref[h] = alpha * l_ref[h] + jnp.sum(p, axis=-1, keepdims=True)
            m_ref[h] = m_new
            pv = jnp.dot(p.astype(BF16), jnp.where(hm, v, 0), preferred_element_type=F32)
            acc = acc * jnp.where(hm, alpha, 1.0) + pv
        acc_ref[...] = acc

    needs_mask = (ki + 1) * tk - 1 > qi * tq

    @pl.when(needs_mask)
    def _():
        body(True)

    @pl.when(jnp.logical_not(needs_mask))
    def _():
        body(False)

    @pl.when((ki + 1) * tk >= (qi + 1) * tq)
    def _():
        inv = jnp.zeros((tq, LANE), F32)
        for h, hm in enumerate(masks):
            inv = jnp.where(hm, 1.0 / l_ref[h], inv)
        o_ref[...] = (acc_ref[...] * inv).astype(BF16)


def _attn_prompt(qkv, fq, fk, b, t, fw, hd, tq, tk):
    npair = fw // LANE
    nq = t // tq
    pairs = [(i, j) for i in range(nq) for j in range(((i + 1) * tq - 1) // tk + 1)]
    qt = jnp.asarray(np.array([p[0] for p in pairs], np.int32))
    kt = jnp.asarray(np.array([p[1] for p in pairs], np.int32))
    nkb = t // tk
    hpp = LANE // hd
    grid_spec = pltpu.PrefetchScalarGridSpec(
        num_scalar_prefetch=2,
        grid=(b, npair, len(pairs)),
        in_specs=[pl.BlockSpec((tq, LANE), lambda bi, p, s, qt, kt: (bi * nq + qt[s], p)),
                  pl.BlockSpec((tk, LANE), lambda bi, p, s, qt, kt: (bi * nkb + kt[s], npair + p)),
                  pl.BlockSpec((tk, LANE), lambda bi, p, s, qt, kt: (bi * nkb + kt[s], 2 * npair + p)),
                  pl.BlockSpec((None, None, tq, hpp), lambda bi, p, s, qt, kt: (bi, p, qt[s], 0)),
                  pl.BlockSpec((None, None, hpp, tk), lambda bi, p, s, qt, kt: (bi, p, 0, kt[s]))],
        out_specs=pl.BlockSpec((tq, LANE), lambda bi, p, s, qt, kt: (bi * nq + qt[s], p)),
        scratch_shapes=[pltpu.VMEM((hpp, tq, 1), F32), pltpu.VMEM((hpp, tq, 1), F32),
                        pltpu.VMEM((tq, LANE), F32)],
    )
    return pl.pallas_call(
        functools.partial(_attn_kernel, tq=tq, tk=tk, hd=hd),
        grid_spec=grid_spec,
        out_shape=jax.ShapeDtypeStruct((b * t, fw), BF16),
        compiler_params=_cparams(("arbitrary", "arbitrary", "arbitrary")),
        name="fox_attn_prompt",
    )(qt, kt, qkv, qkv, qkv, fq, fk)


def _attn_cache_kernel(q_ref, kn_ref, vn_ref, kc_ref, vc_ref, fq_ref, fkc_ref, fkn_ref, o_ref, *, hd):
    masks = _head_masks(hd)
    q, kn, vn = q_ref[...], kn_ref[...], vn_ref[...]
    kc = kc_ref[...].astype(BF16)
    vc = vc_ref[...].astype(BF16)
    s_len = q.shape[0]
    causal = (lax.broadcasted_iota(jnp.int32, (s_len, s_len), 1)
              <= lax.broadcasted_iota(jnp.int32, (s_len, s_len), 0))
    nt = (((1,), (1,)), ((), ()))
    out = jnp.zeros((s_len, LANE), F32)
    for h, hm in enumerate(masks):
        qh = jnp.where(hm, q, 0)
        fq = fq_ref[:, h:h + 1]
        s1 = lax.dot_general(qh, kc, nt, preferred_element_type=F32) + fq - fkc_ref[h:h + 1, :]
        s2 = lax.dot_general(qh, kn, nt, preferred_element_type=F32) + fq - fkn_ref[h:h + 1, :]
        s2 = jnp.where(causal, s2, NEG)
        m = jnp.maximum(jnp.max(s1, axis=-1, keepdims=True), jnp.max(s2, axis=-1, keepdims=True))
        p1 = jnp.exp(s1 - m)
        p2 = jnp.exp(s2 - m)
        den = jnp.sum(p1, axis=-1, keepdims=True) + jnp.sum(p2, axis=-1, keepdims=True)
        pv = (jnp.dot(p1.astype(BF16), jnp.where(hm, vc, 0), preferred_element_type=F32)
              + jnp.dot(p2.astype(BF16), jnp.where(hm, vn, 0), preferred_element_type=F32))
        out = out + pv * (1.0 / den)
    o_ref[...] = out.astype(BF16)


def _attn_cached(qkv, cache_k, cache_v, fq, fkc, fkn, l, b, s_len, fw, hd):
    npair = fw // LANE
    past = cache_k.shape[2]
    hpp = LANE // hd
    cspec = pl.BlockSpec((None, None, past, LANE), lambda bi, p: (l, bi, 0, p))
    return pl.pallas_call(
        functools.partial(_attn_cache_kernel, hd=hd),
        grid=(b, npair),
        in_specs=[pl.BlockSpec((s_len, LANE), lambda bi, p: (bi, p)),
                  pl.BlockSpec((s_len, LANE), lambda bi, p: (bi, npair + p)),
                  pl.BlockSpec((s_len, LANE), lambda bi, p: (bi, 2 * npair + p)),
                  cspec, cspec,
                  pl.BlockSpec((None, None, s_len, hpp), lambda bi, p: (bi, p, 0, 0)),
                  pl.BlockSpec((None, None, hpp, past), lambda bi, p: (bi, p, 0, 0)),
                  pl.BlockSpec((None, None, hpp, s_len), lambda bi, p: (bi, p, 0, 0))],
        out_specs=pl.BlockSpec((s_len, LANE), lambda bi, p: (bi, p)),
        out_shape=jax.ShapeDtypeStruct((b * s_len, fw), BF16),
        compiler_params=_cparams(("arbitrary", "arbitrary")),
        name="fox_attn_cached",
    )(qkv, qkv, qkv, cache_k, cache_v, fq, fkc, fkn)


def _hgrn_kernel(q_ref, f_ref, i_ref, g_ref, lb_ref, gn_ref, s0_ref, o_ref, sfin_ref, st_ref, *, c, nc):
    ci = pl.program_id(2)

    @pl.when(ci == 0)
    def _():
        st_ref[...] = s0_ref[...].astype(F32)

    q = q_ref[...]
    v = i_ref[...]
    lb = lb_ref[...]
    f = lb + (1.0 - lb) * _sigmoid(f_ref[...])
    k = 1.0 - f
    g = jnp.log(f)
    cl, kd = q.shape
    nt = (((1,), (1,)), ((), ()))

    tri = (lax.broadcasted_iota(jnp.int32, (cl, cl), 0)
           >= lax.broadcasted_iota(jnp.int32, (cl, cl), 1)).astype(BF16)
    hi = g.astype(BF16)
    r1 = g - hi.astype(F32)
    mid = r1.astype(BF16)
    lo = (r1 - mid.astype(F32)).astype(BF16)
    b = (jnp.dot(tri, hi, preferred_element_type=F32) + jnp.dot(tri, mid, preferred_element_type=F32)
         + jnp.dot(tri, lo, preferred_element_type=F32))

    st = st_ref[...]
    vb = v.astype(BF16)
    o_inter = jnp.dot((q * jnp.exp(b)).astype(BF16), st.astype(BF16), preferred_element_type=F32)

    sub_s = lax.broadcasted_iota(jnp.int32, (c, c, c), 0)
    sub_t = lax.broadcasted_iota(jnp.int32, (c, c, c), 1)
    sub_n = lax.broadcasted_iota(jnp.int32, (c, c, c), 2)
    pick = (sub_n == sub_s) & (sub_t >= sub_s)
    outs = []
    for i in range(cl // c):
        lo_r, hi_r = i * c, (i + 1) * c
        qi, bi = q[lo_r:hi_r], b[lo_r:hi_r]
        x = jnp.concatenate(
            [qi * jnp.exp(jnp.minimum(bi - b[lo_r + s:lo_r + s + 1], 0.0)) for s in range(c)], axis=0)
        r = lax.dot_general(x.astype(BF16), k[lo_r:hi_r].astype(BF16), nt, preferred_element_type=F32)
        a_diag = jnp.sum(jnp.where(pick, r.reshape(c, c, c), 0.0), axis=0)
        oi = jnp.dot(a_diag.astype(BF16), vb[lo_r:hi_r], preferred_element_type=F32)
        if i > 0:
            ref = b[lo_r - 1:lo_r]
            qd = qi * jnp.exp(bi - ref)
            kdec = k[:lo_r] * jnp.exp(ref - b[:lo_r])
            a_off = lax.dot_general(qd.astype(BF16), kdec.astype(BF16), nt, preferred_element_type=F32)
            oi = oi + jnp.dot(a_off.astype(BF16), vb[:lo_r], preferred_element_type=F32)
        outs.append(oi)
    o = o_inter + jnp.concatenate(outs, axis=0)

    b_last = b[cl - 1:cl]
    e_row = jnp.exp(b_last)
    eye = (lax.broadcasted_iota(jnp.int32, (kd, kd), 0) == lax.broadcasted_iota(jnp.int32, (kd, kd), 1))
    e_col = jnp.sum(jnp.where(eye, e_row, 0.0), axis=1, keepdims=True)
    k_end = (k * jnp.exp(b_last - b)).astype(BF16)
    upd = lax.dot_general(k_end, vb, (((0,), (0,)), ((), ())), preferred_element_type=F32)
    st_new = e_col * st + upd
    st_ref[...] = st_new

    @pl.when(ci == nc - 1)
    def _():
        sfin_ref[...] = st_new

    y = o * lax.rsqrt(jnp.mean(o * o, axis=-1, keepdims=True) + EPS) * gn_ref[...]
    hg = g_ref[...]
    o_ref[...] = (y * (hg * _sigmoid(hg))).astype(BF16)


def _hgrn(proj, lb, gn, s0, b, t, col0, hh, hk):
    cl = _pick(t, 64, HGRN_SUB)
    nc = t // cl

    def col(kind):
        return pl.BlockSpec((cl, hk), lambda bi, h, ci: (bi * nc + ci, col0 + kind * hh + h))

    return pl.pallas_call(
        functools.partial(_hgrn_kernel, c=HGRN_SUB, nc=nc),
        grid=(b, hh, nc),
        in_specs=[col(0), col(1), col(2), col(3),
                  pl.BlockSpec((1, hk), lambda bi, h, ci: (0, h)),
                  pl.BlockSpec((1, hk), lambda bi, h, ci: (0, 0)),
                  pl.BlockSpec((None, None, hk, hk), lambda bi, h, ci: (bi, h, 0, 0))],
        out_specs=[pl.BlockSpec((cl, hk), lambda bi, h, ci: (bi * nc + ci, h)),
                   pl.BlockSpec((None, None, hk, hk), lambda bi, h, ci: (bi, h, 0, 0))],
        out_shape=[jax.ShapeDtypeStruct((b * t, hh * hk), BF16),
                   jax.ShapeDtypeStruct((b, hh, hk, hk), F32)],
        scratch_shapes=[pltpu.VMEM((hk, hk), F32)],
        compiler_params=_cparams(("arbitrary", "arbitrary", "arbitrary")),
        name="hgrn2",
    )(proj, proj, proj, proj, lb, gn, s0)


def _conv_kernel(ga_ref, gb_ref, buf_ref, w_ref, b_ref, lg_ref, lb_ref, o_ref, nb_ref, xs_ref,
                 *, tb, nblk, width, halo, rb):
    t = pl.program_id(1)
    pad = halo - (width - 1)

    @pl.when(t == 0)
    def _():
        xs_ref[0:halo, :] = jnp.zeros((halo, xs_ref.shape[1]), F32)
        xs_ref[pad:halo, :] = buf_ref[...].astype(F32)

    @pl.when(t > 0)
    def _():
        xs_ref[0:halo, :] = xs_ref[tb:tb + halo, :]

    xs_ref[halo:halo + tb, :] = ga_ref[...] * _sigmoid(gb_ref[...])
    for r0 in range(0, tb, rb):
        acc = xs_ref[r0 + pad:r0 + pad + rb, :] * w_ref[0:1, :]
        for w in range(1, width):
            acc = acc + xs_ref[r0 + pad + w:r0 + pad + w + rb, :] * w_ref[w:w + 1, :]
        y = acc + b_ref[...]
        mu = jnp.mean(y, axis=-1, keepdims=True)
        yc = y - mu
        var = jnp.mean(yc * yc, axis=-1, keepdims=True)
        z = yc * lax.rsqrt(var + EPS) * lg_ref[...] + lb_ref[...]
        o_ref[r0:r0 + rb, :] = (z * _sigmoid(z)).astype(BF16)

    @pl.when(t == nblk - 1)
    def _():
        nb_ref[...] = xs_ref[tb + pad:tb + halo, :]


def _conv(proj, buf, dw_w, dw_b, ln_g, ln_b, b, t, ga_blk, cc):
    width = dw_w.shape[0]
    halo = -(-(width - 1) // 8) * 8
    tb = _pick(t, 256, 8)
    nblk = t // tb
    rb = _pick(tb, 64, 8)
    vec = pl.BlockSpec((1, cc), lambda bi, j: (0, 0))
    return pl.pallas_call(
        functools.partial(_conv_kernel, tb=tb, nblk=nblk, width=width, halo=halo, rb=rb),
        grid=(b, nblk),
        in_specs=[pl.BlockSpec((tb, cc), lambda bi, j: (bi * nblk + j, ga_blk)),
                  pl.BlockSpec((tb, cc), lambda bi, j: (bi * nblk + j, ga_blk + 1)),
                  pl.BlockSpec((None, width - 1, cc), lambda bi, j: (bi, 0, 0)),
                  pl.BlockSpec((width, cc), lambda bi, j: (0, 0)),
                  vec, vec, vec],
        out_specs=[pl.BlockSpec((tb, cc), lambda bi, j: (bi * nblk + j, 0)),
                   pl.BlockSpec((None, width - 1, cc), lambda bi, j: (bi, 0, 0))],
        out_shape=[jax.ShapeDtypeStruct((b * t, cc), BF16),
                   jax.ShapeDtypeStruct((b, width - 1, cc), F32)],
        scratch_shapes=[pltpu.VMEM((halo + tb, cc), F32)],
        compiler_params=_cparams(("arbitrary", "arbitrary")),
        name="conv_module",
    )(proj, proj, buf, dw_w, dw_b, ln_g, ln_b)


def _outproj_kernel(x_ref, gt_ref, fox_ref, hg_ref, cv_ref, w1_ref, w2_ref, w3_ref, o_ref):
    acc = (jnp.dot(fox_ref[...], w1_ref[...], preferred_element_type=F32)
           + jnp.dot(hg_ref[...], w2_ref[...], preferred_element_type=F32)
           + jnp.dot(cv_ref[...], w3_ref[...], preferred_element_type=F32))
    o_ref[...] = _gated_residual(x_ref[...], gt_ref[...], acc, 1.0)


def _outproj(x, gate, fox, hg, cv, w_out, l, seq_len, tm):
    m, d = x.shape
    fw, hw, cc = fox.shape[1], hg.shape[1], cv.shape[1]
    tn = _pick(d, 512, LANE)
    grp = max(1, tm // seq_len)
    return pl.pallas_call(
        _outproj_kernel,
        grid=(m // tm, d // tn),
        in_specs=[pl.BlockSpec((tm, tn), lambda i, j: (i, j)),
                  pl.BlockSpec((grp, 1, tn), lambda i, j: ((i * tm) // (seq_len * grp), 0, j)),
                  pl.BlockSpec((tm, fw), lambda i, j: (i, 0)),
                  pl.BlockSpec((tm, hw), lambda i, j: (i, 0)),
                  pl.BlockSpec((tm, cc), lambda i, j: (i, 0)),
                  pl.BlockSpec((None, fw, tn), lambda i, j: (l, 0, j)),
                  pl.BlockSpec((None, hw, tn), lambda i, j: (l, fw // hw, j)),
                  pl.BlockSpec((None, cc, tn), lambda i, j: (l, (fw + hw) // cc, j))],
        out_specs=pl.BlockSpec((tm, tn), lambda i, j: (i, j)),
        out_shape=jax.ShapeDtypeStruct((m, d), F32),
        compiler_params=_cparams(("arbitrary", "arbitrary")),
        name="mix_outproj",
    )(x, gate, fox, hg, cv, w_out, w_out, w_out)


def _norm_kernel(x_ref, g_ref, o_ref):
    x = x_ref[...]
    o_ref[...] = x * lax.rsqrt(jnp.mean(x * x, axis=-1, keepdims=True) + EPS) * g_ref[...]


def _final_norm(x, g, tm):
    m, d = x.shape
    return pl.pallas_call(
        _norm_kernel,
        grid=(m // tm,),
        in_specs=[pl.BlockSpec((tm, d), lambda i: (i, 0)), pl.BlockSpec((1, d), lambda i: (0, 0))],
        out_specs=pl.BlockSpec((tm, d), lambda i: (i, 0)),
        out_shape=jax.ShapeDtypeStruct((m, d), F32),
        compiler_params=_cparams(("arbitrary",)),
        name="final_norm",
    )(x, g)


def _pair_cols(f, npair):
    b, t, h = f.shape
    f4 = f.reshape(b, t, npair, h // npair)
    return f4.transpose(0, 2, 1, 3), f4.transpose(0, 2, 3, 1)


def kernel(x_prompt, x_sample, cache_fox_k, cache_fox_v, cache_fox_logf, state_hgrn, state_conv,
           c_prompt, c_sample, norm_g, ada_w, ada_b, ffn_w_in, ffn_w_out, mix_w_in, mix_w_out,
           fox_f_bias, hgrn_lb_logits, hgrn_norm_g, conv_dw_w, conv_dw_b, conv_ln_g, conv_ln_b,
           final_norm_g):
    bp, seq, d = x_prompt.shape
    bs, dseq, _ = x_sample.shape
    depth = norm_g.shape[0]
    nsub = norm_g.shape[1]
    past = cache_fox_k.shape[2]
    fh, hd = cache_fox_k.shape[3], cache_fox_k.shape[4]
    fw = fh * hd
    hh, hk = state_hgrn.shape[2], state_hgrn.shape[3]
    hw = hh * hk
    cc = conv_dw_b.shape[1]
    width = conv_dw_w.shape[1]
    npair = fw // LANE
    qscale = float(hd) ** -0.5

    lb_sm = jnp.cumsum(jax.nn.softmax(hgrn_lb_logits.astype(F32), axis=0), axis=0)
    lb_all = lb_sm - lb_sm[0:1]

    c_all = jnp.concatenate([c_prompt, c_sample], axis=0)
    rows = bp + bs
    rpad = -(-rows // 16) * 16
    c_act = jnp.pad(jax.nn.silu(c_all), ((0, rpad - rows), (0, 0))).astype(BF16)
    mod_all = _ada(c_act, ada_w, ada_b)[:, :rows].reshape(depth, rows, nsub, 3, 1, d)

    tn_in = _pick(fw, 512, LANE)
    w_in_bf = ffn_w_in.astype(BF16)
    w_out_bf = ffn_w_out.astype(BF16)
    mixo_bf = mix_w_out.astype(BF16)
    n_rest = mix_w_in.shape[2] - 3 * fw - fh
    assert n_rest % tn_in == 0 and fh <= tn_in
    mixi_bf = jnp.concatenate(
        [mix_w_in[:, :, :3 * fw], mix_w_in[:, :, 3 * fw + fh:], mix_w_in[:, :, 3 * fw:3 * fw + fh],
         jnp.zeros((depth, d, tn_in - fh), F32)], axis=2).astype(BF16)
    fz0 = 3 * fw + n_rest
    col_hgrn = 3 * fw // hk
    blk_conv = (3 * fw + 4 * hw) // cc

    cache_k = cache_fox_k.reshape(depth, bs, past, fw)
    cache_v = cache_fox_v.reshape(depth, bs, past, fw)
    zero_bias = jnp.zeros((1, fh), F32)

    def trunk(x, mod, b, t, hist):
        m = b * t
        tm = _pick(m, 1024, 8) if t >= 1024 else _pick(m, 512, t)
        tq = _pick(t, 512, 16)
        new = []
        for l in range(depth):
            md = lambda s, kind: mod[l, :, s, kind]
            x = _ffn(x, norm_g[l, 0][None], md(0, 0), md(0, 1), md(0, 2), w_in_bf, w_out_bf, l, 0, t, tm)
            proj, qkv = _inproj(x, norm_g[l, 1][None], md(1, 0), md(1, 1), mixi_bf, l, fw, tn_in,
                                qscale, t, tm)
            k_new = proj[:, fw:2 * fw].reshape(b, t, fh, hd)
            v_new = proj[:, 2 * fw:3 * fw].reshape(b, t, fh, hd)
            fz = proj[:, fz0:fz0 + fh].reshape(b, t, fh)
            logf, fcum = _logf_cumsum(fz, fox_f_bias[l][None], True)
            if hist is None:
                fq, fk = _pair_cols(fcum, npair)
                fox = _attn_prompt(qkv, fq, fk, b, t, fw, hd, tq, tq)
                s0 = jnp.zeros((b, hh, hk, hk), F32)
                buf = jnp.zeros((b, width - 1, cc), F32)
            else:
                _, fc = _logf_cumsum(cache_fox_logf[l].astype(F32), zero_bias, False)
                fq, fkn = _pair_cols(fcum + fc[:, -1:, :], npair)
                _, fkc = _pair_cols(fc, npair)
                fox = _attn_cached(qkv, cache_k, cache_v, fq, fkc, fkn, l, b, t, fw, hd)
                s0 = state_hgrn[l]
                buf = state_conv[l]
            hg, s_fin = _hgrn(proj, lb_all[l][None], hgrn_norm_g[l][None], s0, b, t, col_hgrn, hh, hk)
            cv, new_buf = _conv(proj, buf, conv_dw_w[l], conv_dw_b[l][None], conv_ln_g[l][None],
                                conv_ln_b[l][None], b, t, blk_conv, cc)
            x = _outproj(x, md(1, 2), fox, hg, cv, mixo_bf, l, t, tm)
            x = _ffn(x, norm_g[l, 2][None], md(2, 0), md(2, 1), md(2, 2), w_in_bf, w_out_bf, l, 1, t, tm)
            new.append((k_new, v_new, logf, s_fin, new_buf))
        y = _final_norm(x, final_norm_g[None], tm).reshape(b, t, d)
        return y, [jnp.stack([s[i] for s in new]) for i in range(5)]

    y_p, (kp, vp, fp, hp, cp) = trunk(x_prompt.reshape(bp * seq, d), mod_all[:, :bp], bp, seq, None)
    y_s, (ks, vs, fs, hs, cs) = trunk(x_sample.reshape(bs * dseq, d), mod_all[:, bp:], bs, dseq, True)
    return (y_p, y_s, kp, vp, fp, hp, cp, ks, vs, fs, hs, cs)
```

```python
import functools

import jax
import jax.numpy as jnp
import numpy as np
from jax import lax
from jax.experimental import pallas as pl
from jax.experimental.pallas import tpu as pltpu

F32 = jnp.float32
BF16 = jnp.bfloat16
EPS = 1e-6
NEG = -1e30
LOG2E = 1.4426950408889634
LANE = 128
ONES_ROWS = 16
HGRN_SUB = 16
FFN_TM, FFN_TF = 512, 512
VMEM_LIMIT = 60 * 1024 * 1024

NT = (((1,), (1,)), ((), ()))
TN = (((0,), (0,)), ((), ()))


def _pick(n, target, mult=8):
    for t in range(min(n, target), 0, -1):
        if n % t == 0 and t % mult == 0:
            return t
    return n


def _cparams(sem):
    return pltpu.CompilerParams(dimension_semantics=sem, vmem_limit_bytes=VMEM_LIMIT)


def _sigmoid(x):
    return 1.0 / (1.0 + jnp.exp(-x))


def _split3(x):
    hi = x.astype(BF16)
    r1 = x - hi.astype(F32)
    mid = r1.astype(BF16)
    lo = (r1 - mid.astype(F32)).astype(BF16)
    return hi, mid, lo


def _lower_tri(n):
    return (lax.broadcasted_iota(jnp.int32, (n, n), 0)
            >= lax.broadcasted_iota(jnp.int32, (n, n), 1)).astype(BF16)


def _modulate(x, g, shift, scale):
    tm, d = x.shape
    grp = shift.shape[0]
    y = x * lax.rsqrt(jnp.mean(x * x, axis=-1, keepdims=True) + EPS) * g
    y = y.reshape(grp, tm // grp, d) * (1.0 + scale) + shift
    return y.reshape(tm, d)


def _gated_residual(x, gate, upd, coef):
    tm, d = x.shape
    grp = gate.shape[0]
    y = x.reshape(grp, tm // grp, d) + (coef * gate) * upd.reshape(grp, tm // grp, d)
    return y.reshape(tm, d)


def _mod_spec(tm, seq_len, d):
    grp = max(1, tm // seq_len)
    return pl.BlockSpec((grp, 1, d), lambda i, j: ((i * tm) // (seq_len * grp), 0, 0))


def _ada_kernel(c_ref, w_ref, b_ref, o_ref):
    w = w_ref[...].astype(BF16)
    o_ref[...] = jnp.dot(c_ref[...], w, preferred_element_type=F32) + b_ref[...]


def _ada(c_act, ada_w, ada_b):
    depth, d, n = ada_w.shape
    r = c_act.shape[0]
    tn = _pick(n, 1024, LANE)
    return pl.pallas_call(
        _ada_kernel,
        grid=(depth, n // tn),
        in_specs=[pl.BlockSpec((r, d), lambda l, j: (0, 0)),
                  pl.BlockSpec((None, d, tn), lambda l, j: (l, 0, j)),
                  pl.BlockSpec((None, 1, tn), lambda l, j: (l, 0, j))],
        out_specs=pl.BlockSpec((None, r, tn), lambda l, j: (l, 0, j)),
        out_shape=jax.ShapeDtypeStruct((depth, r, n), F32),
        compiler_params=_cparams(("arbitrary", "arbitrary")),
        name="ada_proj",
    )(c_act, ada_w, ada_b.reshape(depth, 1, n))


def _ffn_kernel(x_ref, g_ref, sh_ref, sc_ref, gt_ref, wa_ref, wb_ref, wo_ref, o_ref, h_ref, *, nj):
    j = pl.program_id(1)

    @pl.when(j == 0)
    def _():
        h_ref[...] = _modulate(x_ref[...], g_ref[...], sh_ref[...], sc_ref[...]).astype(BF16)
        o_ref[...] = jnp.zeros_like(o_ref)

    h = h_ref[...]
    a = jnp.dot(h, wa_ref[...], preferred_element_type=F32)
    b = jnp.dot(h, wb_ref[...], preferred_element_type=F32)
    act = (a * _sigmoid(a) * b).astype(BF16)
    o_ref[...] += jnp.dot(act, wo_ref[...], preferred_element_type=F32)

    @pl.when(j == nj - 1)
    def _():
        o_ref[...] = _gated_residual(x_ref[...], gt_ref[...], o_ref[...], 0.5)


def _ffn(x, g, shift, scale, gate, w_in, w_out, l, s, seq_len, tm):
    m, d = x.shape
    dff = w_out.shape[2]
    tf = _pick(dff, FFN_TF, LANE)
    nj = dff // tf
    mod = _mod_spec(tm, seq_len, d)
    return pl.pallas_call(
        functools.partial(_ffn_kernel, nj=nj),
        grid=(m // tm, nj),
        in_specs=[pl.BlockSpec((tm, d), lambda i, j: (i, 0)),
                  pl.BlockSpec((1, d), lambda i, j: (0, 0)),
                  mod, mod, mod,
                  pl.BlockSpec((None, None, d, tf), lambda i, j: (l, s, 0, j)),
                  pl.BlockSpec((None, None, d, tf), lambda i, j: (l, s, 0, j + nj)),
                  pl.BlockSpec((None, None, tf, d), lambda i, j: (l, s, j, 0))],
        out_specs=pl.BlockSpec((tm, d), lambda i, j: (i, 0)),
        out_shape=jax.ShapeDtypeStruct((m, d), F32),
        scratch_shapes=[pltpu.VMEM((tm, d), BF16)],
        compiler_params=_cparams(("arbitrary", "arbitrary")),
        name="ffn",
    )(x, g, shift, scale, gate, w_in, w_in, w_out)


def _inproj_kernel(x_ref, g_ref, sh_ref, sc_ref, w_ref, proj_ref, qkv_ref, h_ref, *, nq, nqkv, qscale):
    j = pl.program_id(1)

    @pl.when(j == 0)
    def _():
        h_ref[...] = _modulate(x_ref[...], g_ref[...], sh_ref[...], sc_ref[...]).astype(BF16)

    r = jnp.dot(h_ref[...], w_ref[...], preferred_element_type=F32)
    proj_ref[...] = r

    @pl.when(j < nqkv)
    def _():
        qkv_ref[...] = (r * jnp.where(j < nq, qscale, 1.0)).astype(BF16)


def _inproj(x, g, shift, scale, w_r, l, fw, tn, qscale, seq_len, tm):
    m, d = x.shape
    n_pad = w_r.shape[2]
    nq, nqkv = fw // tn, 3 * fw // tn
    mod = _mod_spec(tm, seq_len, d)
    return pl.pallas_call(
        functools.partial(_inproj_kernel, nq=nq, nqkv=nqkv, qscale=qscale),
        grid=(m // tm, n_pad // tn),
        in_specs=[pl.BlockSpec((tm, d), lambda i, j: (i, 0)),
                  pl.BlockSpec((1, d), lambda i, j: (0, 0)),
                  mod, mod,
                  pl.BlockSpec((None, d, tn), lambda i, j: (l, 0, j))],
        out_specs=[pl.BlockSpec((tm, tn), lambda i, j: (i, j)),
                   pl.BlockSpec((tm, tn), lambda i, j: (i, jnp.minimum(j, nqkv - 1)))],
        out_shape=[jax.ShapeDtypeStruct((m, n_pad), F32),
                   jax.ShapeDtypeStruct((m, 3 * fw), BF16)],
        scratch_shapes=[pltpu.VMEM((tm, d), BF16)],
        compiler_params=_cparams(("arbitrary", "arbitrary")),
        name="mix_inproj",
    )(x, g, shift, scale, w_r)


def _cumsum_kernel(z_ref, bias_ref, lf_ref, f_ref, carry_ref, *, logsig):
    t = pl.program_id(0)

    @pl.when(t == 0)
    def _():
        carry_ref[...] = jnp.zeros_like(carry_ref)

    z = z_ref[...]
    if logsig:
        z = z + bias_ref[...]
        z = jnp.minimum(z, 0.0) - jnp.log(1.0 + jnp.exp(-jnp.abs(z)))
    lf_ref[...] = z
    tb = z.shape[0]
    tri = _lower_tri(tb)
    hi, mid, lo = _split3(z)
    cs = (jnp.dot(tri, hi, preferred_element_type=F32) + jnp.dot(tri, mid, preferred_element_type=F32)
          + jnp.dot(tri, lo, preferred_element_type=F32))
    f = cs + carry_ref[...]
    f_ref[...] = f
    carry_ref[...] = f[tb - 1:tb, :]


def _logf_cumsum(z, bias, logsig):
    t, n = z.shape
    tb = _pick(t, 512, 8)
    spec = pl.BlockSpec((tb, n), lambda i: (i, 0))
    return pl.pallas_call(
        functools.partial(_cumsum_kernel, logsig=logsig),
        grid=(t // tb,),
        in_specs=[spec, pl.BlockSpec((1, n), lambda i: (0, 0))],
        out_specs=[spec, spec],
        out_shape=[jax.ShapeDtypeStruct((t, n), F32)] * 2,
        scratch_shapes=[pltpu.VMEM((1, n), F32)],
        compiler_params=_cparams(("arbitrary",)),
        name="logf_cumsum",
    )(z, bias)


def _attn_kernel(qt_ref, kt_ref, q_ref, k_ref, vt_ref, fq_ref, fk_ref, o_ref,
                 qm_ref, m_ref, acc_ref, *, tq, tk, hd):
    t = pl.program_id(2)
    qi, ki = qt_ref[t], kt_ref[t]
    hpp = LANE // hd
    ones = jnp.ones((ONES_ROWS, tk), BF16)

    @pl.when(ki == 0)
    def _():
        lane = lax.broadcasted_iota(jnp.int32, (1, LANE), 1)
        q = q_ref[...]
        for h in range(hpp):
            qm_ref[h] = jnp.where((lane >= hd * h) & (lane < hd * (h + 1)), q, 0)
        m_ref[...] = jnp.full_like(m_ref, NEG)
        acc_ref[...] = jnp.zeros_like(acc_ref)

    def body(masked):
        kk = k_ref[...]
        if masked:
            kpos = ki * tk + lax.broadcasted_iota(jnp.int32, (tk, tq), 0)
            qpos = qi * tq + lax.broadcasted_iota(jnp.int32, (tk, tq), 1)
            visible = kpos <= qpos
        for h in range(hpp):
            u = (lax.dot_general(kk, qm_ref[h], NT, preferred_element_type=F32)
                 - fk_ref[:, h:h + 1])
            if masked:
                u = jnp.where(visible, u, NEG)
            fq = fq_ref[h:h + 1, :]
            m_prev = m_ref[h]
            m_new = jnp.maximum(m_prev, jnp.max(u, axis=0, keepdims=True) + fq)
            alpha = jnp.exp2(m_prev - m_new)
            p = jnp.exp2(u + (fq - m_new)).astype(BF16)
            m_ref[h] = m_new
            v_one = jnp.concatenate([vt_ref[h * hd:(h + 1) * hd, :], ones], axis=0)
            acc_ref[h] = alpha * acc_ref[h] + jnp.dot(v_one, p, preferred_element_type=F32)

    needs_mask = (ki + 1) * tk - 1 > qi * tq

    @pl.when(needs_mask)
    def _():
        body(True)

    @pl.when(jnp.logical_not(needs_mask))
    def _():
        body(False)

    @pl.when((ki + 1) * tk >= (qi + 1) * tq)
    def _():
        o = jnp.concatenate([acc_ref[h, 0:hd] * (1.0 / acc_ref[h, hd:hd + 1]) for h in range(hpp)],
                            axis=0)
        o_ref[...] = o.T.astype(BF16)


def _attn_prompt(qkv, vt, fq, fk, b, t, fw, hd, tq, tk):
    npair = fw // LANE
    nq, nkb = t // tq, t // tk
    pairs = [(i, j) for i in range(nq) for j in range(((i + 1) * tq - 1) // tk + 1)]
    qt = jnp.asarray(np.array([p[0] for p in pairs], np.int32))
    kt = jnp.asarray(np.array([p[1] for p in pairs], np.int32))
    hpp = LANE // hd
    grid_spec = pltpu.PrefetchScalarGridSpec(
        num_scalar_prefetch=2,
        grid=(b, npair, len(pairs)),
        in_specs=[pl.BlockSpec((tq, LANE), lambda bi, p, s, qt, kt: (bi * nq + qt[s], p)),
                  pl.BlockSpec((tk, LANE), lambda bi, p, s, qt, kt: (bi * nkb + kt[s], npair + p)),
                  pl.BlockSpec((LANE, tk), lambda bi, p, s, qt, kt: (p, bi * nkb + kt[s])),
                  pl.BlockSpec((None, None, hpp, tq), lambda bi, p, s, qt, kt: (bi, p, 0, qt[s])),
                  pl.BlockSpec((None, None, tk, hpp), lambda bi, p, s, qt, kt: (bi, p, kt[s], 0))],
        out_specs=pl.BlockSpec((tq, LANE), lambda bi, p, s, qt, kt: (bi * nq + qt[s], p)),
        scratch_shapes=[pltpu.VMEM((hpp, tq, LANE), BF16),
                        pltpu.VMEM((hpp, 1, tq), F32),
                        pltpu.VMEM((hpp, hd + ONES_ROWS, tq), F32)],
    )
    return pl.pallas_call(
        functools.partial(_attn_kernel, tq=tq, tk=tk, hd=hd),
        grid_spec=grid_spec,
        out_shape=jax.ShapeDtypeStruct((b * t, fw), BF16),
        compiler_params=_cparams(("arbitrary", "arbitrary", "arbitrary")),
        name="fox_attn_prompt",
    )(qt, kt, qkv, qkv, vt, fq, fk)


def _pair_layouts(f2, b, fh, hd):
    t = f2.shape[0]
    hpp = LANE // hd
    f4 = f2.reshape(t, b, fh // hpp, hpp)
    return f4.transpose(1, 2, 3, 0), f4.transpose(1, 2, 0, 3)


def _attn_cache_kernel(qkv_ref, kc_ref, vc_ref, fq_ref, fkc_ref, fkn_ref, o_ref, m_ref, l_ref, acc_ref,
                       *, fh, hd, tk, nkb):
    kb = pl.program_id(1)
    fw = fh * hd
    s_len = qkv_ref.shape[0]

    @pl.when(kb == 0)
    def _():
        m_ref[...] = jnp.full_like(m_ref, NEG)
        l_ref[...] = jnp.zeros_like(l_ref)
        acc_ref[...] = jnp.zeros_like(acc_ref)

    def update(keys, vals, fk_ref, mask):
        s = jnp.concatenate(
            [lax.dot_general(qkv_ref[:, h * hd:(h + 1) * hd], keys[h], NT, preferred_element_type=F32)
             + (fq_ref[:, h:h + 1] - fk_ref[h:h + 1, :]) for h in range(fh)], axis=0)
        if mask is not None:
            s = jnp.where(mask, s, NEG)
        m_prev = m_ref[...]
        m_new = jnp.maximum(m_prev, jnp.max(s, axis=-1, keepdims=True))
        alpha = jnp.exp2(m_prev - m_new)
        p = jnp.exp2(s - m_new)
        l_ref[...] = alpha * l_ref[...] + jnp.sum(p, axis=-1, keepdims=True)
        m_ref[...] = m_new
        pb = p.astype(BF16)
        pv = jnp.concatenate(
            [jnp.dot(pb[h * s_len:(h + 1) * s_len], vals[h], preferred_element_type=F32)
             for h in range(fh)], axis=0)
        acc_ref[...] = alpha * acc_ref[...] + pv

    update([kc_ref[pl.ds(h, tk, stride=fh), :].astype(BF16) for h in range(fh)],
           [vc_ref[pl.ds(h, tk, stride=fh), :].astype(BF16) for h in range(fh)], fkc_ref, None)

    @pl.when(kb == nkb - 1)
    def _():
        row = lax.broadcasted_iota(jnp.int32, (fh * s_len, s_len), 0)
        col = lax.broadcasted_iota(jnp.int32, (fh * s_len, s_len), 1)
        update([qkv_ref[:, fw + h * hd:fw + (h + 1) * hd] for h in range(fh)],
               [qkv_ref[:, 2 * fw + h * hd:2 * fw + (h + 1) * hd] for h in range(fh)], fkn_ref,
               col <= row % s_len)
        o = acc_ref[...] * (1.0 / l_ref[...])
        o_ref[...] = jnp.concatenate([o[h * s_len:(h + 1) * s_len] for h in range(fh)], axis=1).astype(BF16)


def _attn_cached(qkv, cache_k, cache_v, fq, fkc, fkn, l, b, s_len, fh, hd):
    fw = fh * hd
    past = cache_k.shape[2] // fh
    tk = _pick(past, 512, 8)
    nkb = past // tk
    cspec = pl.BlockSpec((None, None, tk * fh, hd), lambda bi, kb: (l, bi, kb, 0))
    return pl.pallas_call(
        functools.partial(_attn_cache_kernel, fh=fh, hd=hd, tk=tk, nkb=nkb),
        grid=(b, nkb),
        in_specs=[pl.BlockSpec((s_len, 3 * fw), lambda bi, kb: (bi, 0)),
                  cspec, cspec,
                  pl.BlockSpec((None, s_len, fh), lambda bi, kb: (bi, 0, 0)),
                  pl.BlockSpec((None, fh, tk), lambda bi, kb: (bi, 0, kb)),
                  pl.BlockSpec((None, fh, s_len), lambda bi, kb: (bi, 0, 0))],
        out_specs=pl.BlockSpec((s_len, fw), lambda bi, kb: (bi, 0)),
        out_shape=jax.ShapeDtypeStruct((b * s_len, fw), BF16),
        scratch_shapes=[pltpu.VMEM((fh * s_len, 1), F32), pltpu.VMEM((fh * s_len, 1), F32),
                        pltpu.VMEM((fh * s_len, hd), F32)],
        compiler_params=_cparams(("arbitrary", "arbitrary")),
        name="fox_attn_cached",
    )(qkv, cache_k, cache_v, fq, fkc, fkn)


def _hgrn_kernel(q_ref, f_ref, i_ref, g_ref, lb_ref, gn_ref, s0_ref, o_ref, sfin_ref, st_ref,
                 *, c, nc, hh, hk):
    ci = pl.program_id(1)

    @pl.when(ci == 0)
    def _():
        for h in range(hh):
            st_ref[h] = s0_ref[h].astype(F32).T

    lb = lb_ref[...]
    f = lb + (1.0 - lb) * _sigmoid(f_ref[...])
    k_all = 1.0 - f
    g = jnp.log(f)
    cl = g.shape[0]
    hw = hh * hk

    b3 = jnp.dot(_lower_tri(cl), jnp.concatenate(_split3(g), axis=1), preferred_element_type=F32)
    b_all = b3[:, :hw] + b3[:, hw:2 * hw] + b3[:, 2 * hw:]
    q_all, v_all, gate_all = q_ref[...], i_ref[...], g_ref[...]

    sub_s = lax.broadcasted_iota(jnp.int32, (c, c, c), 0)
    sub_t = lax.broadcasted_iota(jnp.int32, (c, c, c), 1)
    sub_n = lax.broadcasted_iota(jnp.int32, (c, c, c), 2)
    pick = (sub_n == sub_s) & (sub_t >= sub_s)
    ys = []
    for h in range(hh):
        sl = slice(h * hk, (h + 1) * hk)
        q, k, v, b = q_all[:, sl], k_all[:, sl], v_all[:, sl], b_all[:, sl]
        vb = v.astype(BF16)
        st = st_ref[h]
        o_inter = lax.dot_general((q * jnp.exp(b)).astype(BF16), st.astype(BF16), NT,
                                  preferred_element_type=F32)
        outs = []
        for i in range(cl // c):
            lo_r, hi_r = i * c, (i + 1) * c
            qi, bi = q[lo_r:hi_r], b[lo_r:hi_r]
            x = jnp.concatenate(
                [qi * jnp.exp(jnp.minimum(bi - b[lo_r + s:lo_r + s + 1], 0.0)) for s in range(c)], axis=0)
            r = lax.dot_general(x.astype(BF16), k[lo_r:hi_r].astype(BF16), NT, preferred_element_type=F32)
            a_diag = jnp.sum(jnp.where(pick, r.reshape(c, c, c), 0.0), axis=0)
            oi = jnp.dot(a_diag.astype(BF16), vb[lo_r:hi_r], preferred_element_type=F32)
            if i > 0:
                ref = b[lo_r - 1:lo_r]
                qd = qi * jnp.exp(bi - ref)
                kdec = k[:lo_r] * jnp.exp(ref - b[:lo_r])
                a_off = lax.dot_general(qd.astype(BF16), kdec.astype(BF16), NT, preferred_element_type=F32)
                oi = oi + jnp.dot(a_off.astype(BF16), vb[:lo_r], preferred_element_type=F32)
            outs.append(oi)
        o = o_inter + jnp.concatenate(outs, axis=0)

        b_last = b[cl - 1:cl]
        k_end = (k * jnp.exp(b_last - b)).astype(BF16)
        st_ref[h] = st * jnp.exp(b_last) + lax.dot_general(vb, k_end, TN, preferred_element_type=F32)

        y = o * lax.rsqrt(jnp.mean(o * o, axis=-1, keepdims=True) + EPS) * gn_ref[...]
        gate = gate_all[:, sl]
        ys.append(y * (gate * _sigmoid(gate)))
    o_ref[...] = jnp.concatenate(ys, axis=1).astype(BF16)

    @pl.when(ci == nc - 1)
    def _():
        for h in range(hh):
            sfin_ref[h] = st_ref[h].T


def _hgrn(proj, lb, gn, s0, b, t, col0, hh, hk):
    cl = _pick(t, 64, HGRN_SUB)
    nc = t // cl
    hw = hh * hk

    def col(kind):
        return pl.BlockSpec((cl, hw), lambda bi, ci: (bi * nc + ci, col0 + kind))

    return pl.pallas_call(
        functools.partial(_hgrn_kernel, c=HGRN_SUB, nc=nc, hh=hh, hk=hk),
        grid=(b, nc),
        in_specs=[col(0), col(1), col(2), col(3),
                  pl.BlockSpec((1, hw), lambda bi, ci: (0, 0)),
                  pl.BlockSpec((1, hk), lambda bi, ci: (0, 0)),
                  pl.BlockSpec((None, hh, hk, hk), lambda bi, ci: (bi, 0, 0, 0))],
        out_specs=[pl.BlockSpec((cl, hw), lambda bi, ci: (bi * nc + ci, 0)),
                   pl.BlockSpec((None, hh, hk, hk), lambda bi, ci: (bi, 0, 0, 0))],
        out_shape=[jax.ShapeDtypeStruct((b * t, hw), BF16),
                   jax.ShapeDtypeStruct((b, hh, hk, hk), F32)],
        scratch_shapes=[pltpu.VMEM((hh, hk, hk), F32)],
        compiler_params=_cparams(("arbitrary", "arbitrary")),
        name="hgrn2",
    )(proj, proj, proj, proj, lb, gn, s0)


def _conv_kernel(ga_ref, gb_ref, buf_ref, w_ref, b_ref, lg_ref, lb_ref, o_ref, nb_ref, xs_ref,
                 *, tb, nblk, width, halo, rb):
    t = pl.program_id(1)
    pad = halo - (width - 1)

    @pl.when(t == 0)
    def _():
        xs_ref[0:halo, :] = jnp.zeros((halo, xs_ref.shape[1]), F32)
        xs_ref[pad:halo, :] = buf_ref[...].astype(F32)

    @pl.when(t > 0)
    def _():
        xs_ref[0:halo, :] = xs_ref[tb:tb + halo, :]

    xs_ref[halo:halo + tb, :] = ga_ref[...] * _sigmoid(gb_ref[...])
    for r0 in range(0, tb, rb):
        acc = xs_ref[r0 + pad:r0 + pad + rb, :] * w_ref[0:1, :]
        for w in range(1, width):
            acc = acc + xs_ref[r0 + pad + w:r0 + pad + w + rb, :] * w_ref[w:w + 1, :]
        y = acc + b_ref[...]
        mu = jnp.mean(y, axis=-1, keepdims=True)
        yc = y - mu
        var = jnp.mean(yc * yc, axis=-1, keepdims=True)
        z = yc * lax.rsqrt(var + EPS) * lg_ref[...] + lb_ref[...]
        o_ref[r0:r0 + rb, :] = (z * _sigmoid(z)).astype(BF16)

    @pl.when(t == nblk - 1)
    def _():
        nb_ref[...] = xs_ref[tb + pad:tb + halo, :]


def _conv(proj, buf, dw_w, dw_b, ln_g, ln_b, b, t, ga_blk, cc):
    width = dw_w.shape[0]
    halo = -(-(width - 1) // 8) * 8
    tb = _pick(t, 256, 8)
    nblk = t // tb
    rb = _pick(tb, 64, 8)
    vec = pl.BlockSpec((1, cc), lambda bi, j: (0, 0))
    return pl.pallas_call(
        functools.partial(_conv_kernel, tb=tb, nblk=nblk, width=width, halo=halo, rb=rb),
        grid=(b, nblk),
        in_specs=[pl.BlockSpec((tb, cc), lambda bi, j: (bi * nblk + j, ga_blk)),
                  pl.BlockSpec((tb, cc), lambda bi, j: (bi * nblk + j, ga_blk + 1)),
                  pl.BlockSpec((None, width - 1, cc), lambda bi, j: (bi, 0, 0)),
                  pl.BlockSpec((width, cc), lambda bi, j: (0, 0)),
                  vec, vec, vec],
        out_specs=[pl.BlockSpec((tb, cc), lambda bi, j: (bi * nblk + j, 0)),
                   pl.BlockSpec((None, width - 1, cc), lambda bi, j: (bi, 0, 0))],
        out_shape=[jax.ShapeDtypeStruct((b * t, cc), BF16),
                   jax.ShapeDtypeStruct((b, width - 1, cc), F32)],
        scratch_shapes=[pltpu.VMEM((halo + tb, cc), F32)],
        compiler_params=_cparams(("arbitrary", "arbitrary")),
        name="conv_module",
    )(proj, proj, buf, dw_w, dw_b, ln_g, ln_b)


def _outproj_kernel(x_ref, gt_ref, fox_ref, hg_ref, cv_ref, w1_ref, w2_ref, w3_ref, o_ref):
    acc = (jnp.dot(fox_ref[...], w1_ref[...], preferred_element_type=F32)
           + jnp.dot(hg_ref[...], w2_ref[...], preferred_element_type=F32)
           + jnp.dot(cv_ref[...], w3_ref[...], preferred_element_type=F32))
    o_ref[...] = _gated_residual(x_ref[...], gt_ref[...], acc, 1.0)


def _outproj(x, gate, fox, hg, cv, w_out, l, seq_len, tm):
    m, d = x.shape
    fw, hw, cc = fox.shape[1], hg.shape[1], cv.shape[1]
    tn = _pick(d, 512, LANE)
    grp = max(1, tm // seq_len)
    return pl.pallas_call(
        _outproj_kernel,
        grid=(m // tm, d // tn),
        in_specs=[pl.BlockSpec((tm, tn), lambda i, j: (i, j)),
                  pl.BlockSpec((grp, 1, tn), lambda i, j: ((i * tm) // (seq_len * grp), 0, j)),
                  pl.BlockSpec((tm, fw), lambda i, j: (i, 0)),
                  pl.BlockSpec((tm, hw), lambda i, j: (i, 0)),
                  pl.BlockSpec((tm, cc), lambda i, j: (i, 0)),
                  pl.BlockSpec((None, fw, tn), lambda i, j: (l, 0, j)),
                  pl.BlockSpec((None, hw, tn), lambda i, j: (l, fw // hw, j)),
                  pl.BlockSpec((None, cc, tn), lambda i, j: (l, (fw + hw) // cc, j))],
        out_specs=pl.BlockSpec((tm, tn), lambda i, j: (i, j)),
        out_shape=jax.ShapeDtypeStruct((m, d), F32),
        compiler_params=_cparams(("arbitrary", "arbitrary")),
        name="mix_outproj",
    )(x, gate, fox, hg, cv, w_out, w_out, w_out)


def _norm_kernel(x_ref, g_ref, o_ref):
    x = x_ref[...]
    o_ref[...] = x * lax.rsqrt(jnp.mean(x * x, axis=-1, keepdims=True) + EPS) * g_ref[...]


def _final_norm(x, g, tm):
    m, d = x.shape
    return pl.pallas_call(
        _norm_kernel,
        grid=(m // tm,),
        in_specs=[pl.BlockSpec((tm, d), lambda i: (i, 0)), pl.BlockSpec((1, d), lambda i: (0, 0))],
        out_specs=pl.BlockSpec((tm, d), lambda i: (i, 0)),
        out_shape=jax.ShapeDtypeStruct((m, d), F32),
        compiler_params=_cparams(("arbitrary",)),
        name="final_norm",
    )(x, g)


def _time_major(a, b, t):
    h = a.shape[-1]
    return a.reshape(b, t, h).transpose(1, 0, 2).reshape(t, b * h)


def _batch_major(a, b):
    t = a.shape[0]
    return a.reshape(t, b, -1).transpose(1, 0, 2)


def kernel(x_prompt, x_sample, cache_fox_k, cache_fox_v, cache_fox_logf, state_hgrn, state_conv,
           c_prompt, c_sample, norm_g, ada_w, ada_b, ffn_w_in, ffn_w_out, mix_w_in, mix_w_out,
           fox_f_bias, hgrn_lb_logits, hgrn_norm_g, conv_dw_w, conv_dw_b, conv_ln_g, conv_ln_b,
           final_norm_g):
    bp, seq, d = x_prompt.shape
    bs, dseq, _ = x_sample.shape
    depth = norm_g.shape[0]
    nsub = norm_g.shape[1]
    past = cache_fox_k.shape[2]
    fh, hd = cache_fox_k.shape[3], cache_fox_k.shape[4]
    fw = fh * hd
    hh, hk = state_hgrn.shape[2], state_hgrn.shape[3]
    hw = hh * hk
    cc = conv_dw_b.shape[1]
    width = conv_dw_w.shape[1]
    qscale = float(hd) ** -0.5 * LOG2E

    lb_sm = jnp.cumsum(jax.nn.softmax(hgrn_lb_logits.astype(F32), axis=0), axis=0)
    lb_all = lb_sm - lb_sm[0:1]

    c_all = jnp.concatenate([c_prompt, c_sample], axis=0)
    rows = bp + bs
    rpad = -(-rows // 16) * 16
    c_act = jnp.pad(jax.nn.silu(c_all), ((0, rpad - rows), (0, 0))).astype(BF16)
    mod_all = _ada(c_act, ada_w, ada_b)[:, :rows].reshape(depth, rows, nsub, 3, 1, d)

    tn_in = _pick(fw, 512, LANE)
    w_in_bf = ffn_w_in.astype(BF16)
    w_out_bf = ffn_w_out.astype(BF16)
    mixo_bf = mix_w_out.astype(BF16)
    n_rest = mix_w_in.shape[2] - 3 * fw - fh
    assert n_rest % tn_in == 0 and fh <= tn_in
    mixi_bf = jnp.concatenate(
        [mix_w_in[:, :, :3 * fw], mix_w_in[:, :, 3 * fw + fh:], mix_w_in[:, :, 3 * fw:3 * fw + fh],
         jnp.zeros((depth, d, tn_in - fh), F32)], axis=2).astype(BF16)
    fz0 = 3 * fw + n_rest
    blk_hgrn = 3 * fw // hw
    blk_conv = (3 * fw + 4 * hw) // cc

    cache_k = cache_fox_k.reshape(depth, bs, past * fh, hd)
    cache_v = cache_fox_v.reshape(depth, bs, past * fh, hd)

    def trunk(x, mod, b, t, cached):
        m = b * t
        tm = _pick(m, 1024, 8) if t >= 1024 else _pick(m, 512, t)
        tm_ffn = _pick(m, FFN_TM, 8) if t >= FFN_TM else tm
        tq = _pick(t, 1024, LANE)
        bias_row = lambda l: jnp.tile(fox_f_bias[l], b)[None]
        new = []
        for l in range(depth):
            md = lambda s, kind: mod[l, :, s, kind]
            x = _ffn(x, norm_g[l, 0][None], md(0, 0), md(0, 1), md(0, 2), w_in_bf, w_out_bf, l, 0, t,
                     tm_ffn)
            proj, qkv = _inproj(x, norm_g[l, 1][None], md(1, 0), md(1, 1), mixi_bf, l, fw, tn_in,
                                qscale, t, tm)
            k_new = proj[:, fw:2 * fw].reshape(b, t, fh, hd)
            v_new = proj[:, 2 * fw:3 * fw].reshape(b, t, fh, hd)
            logf_tm, fcum = _logf_cumsum(_time_major(proj[:, fz0:fz0 + fh], b, t), bias_row(l), True)
            logf = _batch_major(logf_tm, b)
            if not cached:
                fq, fk = _pair_layouts(fcum * LOG2E, b, fh, hd)
                fox = _attn_prompt(qkv, qkv[:, 2 * fw:].T, fq, fk, b, t, fw, hd, tq, tq)
                s0 = jnp.zeros((b, hh, hk, hk), F32)
                buf = jnp.zeros((b, width - 1, cc), F32)
            else:
                cl_tm = cache_fox_logf[l].astype(F32).transpose(1, 0, 2).reshape(past, b * fh)
                _, fc = _logf_cumsum(cl_tm, jnp.zeros((1, b * fh), F32), False)
                f2_new = _batch_major((fcum + fc[-1:]) * LOG2E, b)
                f2_old = _batch_major(fc * LOG2E, b)
                fox = _attn_cached(qkv, cache_k, cache_v, f2_new, f2_old.transpose(0, 2, 1),
                                   f2_new.transpose(0, 2, 1), l, b, t, fh, hd)
                s0 = state_hgrn[l]
                buf = state_conv[l]
            hg, s_fin = _hgrn(proj, lb_all[l][None], hgrn_norm_g[l][None], s0, b, t, blk_hgrn, hh, hk)
            cv, new_buf = _conv(proj, buf, conv_dw_w[l], conv_dw_b[l][None], conv_ln_g[l][None],
                                conv_ln_b[l][None], b, t, blk_conv, cc)
            x = _outproj(x, md(1, 2), fox, hg, cv, mixo_bf, l, t, tm)
            x = _ffn(x, norm_g[l, 2][None], md(2, 0), md(2, 1), md(2, 2), w_in_bf, w_out_bf, l, 1, t,
                     tm_ffn)
            new.append((k_new, v_new, logf, s_fin, new_buf))
        y = _final_norm(x, final_norm_g[None], tm).reshape(b, t, d)
        return y, [jnp.stack([s[i] for s in new]) for i in range(5)]

    y_p, (kp, vp, fp, hp, cp) = trunk(x_prompt.reshape(bp * seq, d), mod_all[:, :bp], bp, seq, False)
    y_s, (ks, vs, fs, hs, cs) = trunk(x_sample.reshape(bs * dseq, d), mod_all[:, bp:], bs, dseq, True)
    return (y_p, y_s, kp, vp, fp, hp, cp, ks, vs, fs, hs, cs)
```

```python
import functools

import jax
import jax.numpy as jnp
import numpy as np
from jax import lax
from jax.experimental import pallas as pl
from jax.experimental.pallas import tpu as pltpu

F32 = jnp.float32
BF16 = jnp.bfloat16
EPS = 1e-6
NEG = -1e30
LOG2E = 1.4426950408889634
LANE = 128
ONES_ROWS = 16
CACHE_HEAD_GROUP = 4
HGRN_SUB = 16
FFN_TM, FFN_TF = 512, 512
FFN_TF_EMIT = 256
VMEM_LIMIT = 60 * 1024 * 1024

NT = (((1,), (1,)), ((), ()))
TN = (((0,), (0,)), ((), ()))


def _pick(n, target, mult=8):
    for t in range(min(n, target), 0, -1):
        if n % t == 0 and t % mult == 0:
            return t
    return n


def _cparams(sem):
    return pltpu.CompilerParams(dimension_semantics=sem, vmem_limit_bytes=VMEM_LIMIT)


def _sigmoid(x):
    return 1.0 / (1.0 + jnp.exp(-x))


def _split3(x):
    hi = x.astype(BF16)
    r1 = x - hi.astype(F32)
    mid = r1.astype(BF16)
    lo = (r1 - mid.astype(F32)).astype(BF16)
    return hi, mid, lo


def _lower_tri(n):
    return (lax.broadcasted_iota(jnp.int32, (n, n), 0)
            >= lax.broadcasted_iota(jnp.int32, (n, n), 1)).astype(BF16)


def _modulate(x, g, shift, scale):
    tm, d = x.shape
    grp = shift.shape[0]
    y = x * lax.rsqrt(jnp.mean(x * x, axis=-1, keepdims=True) + EPS) * g
    y = y.reshape(grp, tm // grp, d) * (1.0 + scale) + shift
    return y.reshape(tm, d)


def _gated_residual(x, gate, upd, coef):
    tm, d = x.shape
    grp = gate.shape[0]
    y = x.reshape(grp, tm // grp, d) + (coef * gate) * upd.reshape(grp, tm // grp, d)
    return y.reshape(tm, d)


def _mod_spec(tm, seq_len, tn, where, tiled=False):
    l, s, kind, r0 = where
    grp = max(1, tm // seq_len)
    assert r0 % grp == 0
    return pl.BlockSpec(
        (None, grp, None, None, 1, tn),
        lambda i, j: (l, r0 // grp + (i * tm) // (seq_len * grp), s, kind, 0, j if tiled else 0))


def _vec_spec(n, idx):
    return pl.BlockSpec((None, 1, n), lambda *_: (idx, 0, 0))


def _ada_kernel(c_ref, w_ref, b_ref, o_ref):
    w = w_ref[...].astype(BF16)
    o_ref[...] = jnp.dot(c_ref[...], w, preferred_element_type=F32) + b_ref[...]


def _ada(c_act, ada_w, ada_b):
    depth, d, n = ada_w.shape
    r = c_act.shape[0]
    tn = _pick(n, 1024, LANE)
    return pl.pallas_call(
        _ada_kernel,
        grid=(depth, n // tn),
        in_specs=[pl.BlockSpec((r, d), lambda l, j: (0, 0)),
                  pl.BlockSpec((None, d, tn), lambda l, j: (l, 0, j)),
                  pl.BlockSpec((None, 1, tn), lambda l, j: (l, 0, j))],
        out_specs=pl.BlockSpec((None, r, tn), lambda l, j: (l, 0, j)),
        out_shape=jax.ShapeDtypeStruct((depth, r, n), F32),
        compiler_params=_cparams(("arbitrary", "arbitrary")),
        name="ada_proj",
    )(c_act, ada_w, ada_b.reshape(depth, 1, n))


def _ffn_kernel(x_ref, g_ref, sh_ref, sc_ref, gt_ref, wa_ref, wb_ref, wo_ref, o_ref, *rest, nj, emit):
    if emit:
        wab_ref, wbb_ref, wob_ref, h_ref = rest
    else:
        (h_ref,) = rest
    j = pl.program_id(1)

    @pl.when(j == 0)
    def _():
        h_ref[...] = _modulate(x_ref[...], g_ref[...], sh_ref[...], sc_ref[...]).astype(BF16)
        o_ref[...] = jnp.zeros_like(o_ref)

    wa, wb, wo = wa_ref[...], wb_ref[...], wo_ref[...]
    if emit:
        wa, wb, wo = wa.astype(BF16), wb.astype(BF16), wo.astype(BF16)
        wab_ref[...] = wa
        wbb_ref[...] = wb
        wob_ref[...] = wo
    h = h_ref[...]
    a = jnp.dot(h, wa, preferred_element_type=F32)
    b = jnp.dot(h, wb, preferred_element_type=F32)
    act = (a * _sigmoid(a) * b).astype(BF16)
    o_ref[...] += jnp.dot(act, wo, preferred_element_type=F32)

    @pl.when(j == nj - 1)
    def _():
        o_ref[...] = _gated_residual(x_ref[...], gt_ref[...], o_ref[...], 0.5)


def _ffn(x, g3, mod6, wa, wb, wo, l, s, r0, seq_len, tm, emit):
    m, d = x.shape
    dff = wo.shape[-2]
    tf = _pick(dff, FFN_TF_EMIT if emit else FFN_TF, LANE)
    nj = dff // tf
    nsub = mod6.shape[2]
    mods = [_mod_spec(tm, seq_len, d, (l, s, kind, r0)) for kind in range(3)]
    if emit:
        assert m == tm
        w_specs = [pl.BlockSpec((None, None, d, tf), lambda i, j: (l, s // 2, 0, j)),
                   pl.BlockSpec((None, None, d, tf), lambda i, j: (l, s // 2, 0, j + nj)),
                   pl.BlockSpec((None, None, tf, d), lambda i, j: (l, s // 2, j, 0))]
    bf_specs = [pl.BlockSpec((d, tf), lambda i, j: (0, j)),
                pl.BlockSpec((d, tf), lambda i, j: (0, j)),
                pl.BlockSpec((tf, d), lambda i, j: (j, 0))]
    if not emit:
        w_specs = bf_specs
    out_specs = [pl.BlockSpec((tm, d), lambda i, j: (i, 0))]
    out_shape = [jax.ShapeDtypeStruct((m, d), F32)]
    if emit:
        out_specs += bf_specs
        out_shape += [jax.ShapeDtypeStruct((d, dff), BF16)] * 2 + [jax.ShapeDtypeStruct((dff, d), BF16)]
    return pl.pallas_call(
        functools.partial(_ffn_kernel, nj=nj, emit=emit),
        grid=(m // tm, nj),
        in_specs=[pl.BlockSpec((tm, d), lambda i, j: (i, 0)), _vec_spec(d, l * nsub + s)] + mods + w_specs,
        out_specs=out_specs,
        out_shape=out_shape,
        scratch_shapes=[pltpu.VMEM((tm, d), BF16)],
        compiler_params=_cparams(("arbitrary", "arbitrary")),
        name="ffn",
    )(x, g3, mod6, mod6, mod6, wa, wb, wo)


def _inproj_kernel(x_ref, g_ref, sh_ref, sc_ref, wq_ref, wr_ref, proj_ref, qkv_ref, h_ref,
                   *, nq, nqkv, qscale):
    j = pl.program_id(1)

    @pl.when(j == 0)
    def _():
        h_ref[...] = _modulate(x_ref[...], g_ref[...], sh_ref[...], sc_ref[...]).astype(BF16)

    @pl.when(j < nqkv)
    def _():
        r = jnp.dot(h_ref[...], wq_ref[...].astype(BF16), preferred_element_type=F32)
        proj_ref[...] = r
        qkv_ref[...] = (r * jnp.where(j < nq, qscale, 1.0)).astype(BF16)

    @pl.when(j >= nqkv)
    def _():
        proj_ref[...] = jnp.dot(h_ref[...], wr_ref[...].astype(BF16), preferred_element_type=F32)


def _inproj(x, g3, mod6, w_in, w_rest, l, s, r0, fw, tn, qscale, seq_len, tm):
    m, d = x.shape
    nq, nqkv = fw // tn, 3 * fw // tn
    n_pad = 3 * fw + w_rest.shape[2]
    nsub = mod6.shape[2]
    return pl.pallas_call(
        functools.partial(_inproj_kernel, nq=nq, nqkv=nqkv, qscale=qscale),
        grid=(m // tm, n_pad // tn),
        in_specs=[pl.BlockSpec((tm, d), lambda i, j: (i, 0)), _vec_spec(d, l * nsub + s),
                  _mod_spec(tm, seq_len, d, (l, s, 0, r0)), _mod_spec(tm, seq_len, d, (l, s, 1, r0)),
                  pl.BlockSpec((None, d, tn), lambda i, j: (l, 0, jnp.minimum(j, nqkv - 1))),
                  pl.BlockSpec((None, d, tn), lambda i, j: (l, 0, jnp.maximum(j - nqkv, 0)))],
        out_specs=[pl.BlockSpec((tm, tn), lambda i, j: (i, j)),
                   pl.BlockSpec((tm, tn), lambda i, j: (i, jnp.minimum(j, nqkv - 1)))],
        out_shape=[jax.ShapeDtypeStruct((m, n_pad), F32),
                   jax.ShapeDtypeStruct((m, 3 * fw), BF16)],
        scratch_shapes=[pltpu.VMEM((tm, d), BF16)],
        compiler_params=_cparams(("arbitrary", "arbitrary")),
        name="mix_inproj",
    )(x, g3, mod6, mod6, w_in, w_rest)


def _cumsum_kernel(z_ref, bias_ref, lf_ref, f_ref, carry_ref, *, logsig):
    t = pl.program_id(0)

    @pl.when(t == 0)
    def _():
        carry_ref[...] = jnp.zeros_like(carry_ref)

    z = z_ref[...]
    if logsig:
        z = z + bias_ref[...]
        z = jnp.minimum(z, 0.0) - jnp.log(1.0 + jnp.exp(-jnp.abs(z)))
    lf_ref[...] = z
    tb = z.shape[0]
    tri = _lower_tri(tb)
    hi, mid, lo = _split3(z)
    cs = (jnp.dot(tri, hi, preferred_element_type=F32) + jnp.dot(tri, mid, preferred_element_type=F32)
          + jnp.dot(tri, lo, preferred_element_type=F32))
    f = cs + carry_ref[...]
    f_ref[...] = f
    carry_ref[...] = f[tb - 1:tb, :]


def _logf_cumsum(z, bias, logsig):
    t, n = z.shape
    tb = _pick(t, 512, 8)
    spec = pl.BlockSpec((tb, n), lambda i: (i, 0))
    return pl.pallas_call(
        functools.partial(_cumsum_kernel, logsig=logsig),
        grid=(t // tb,),
        in_specs=[spec, pl.BlockSpec((1, n), lambda i: (0, 0))],
        out_specs=[spec, spec],
        out_shape=[jax.ShapeDtypeStruct((t, n), F32)] * 2,
        scratch_shapes=[pltpu.VMEM((1, n), F32)],
        compiler_params=_cparams(("arbitrary",)),
        name="logf_cumsum",
    )(z, bias)


def _attn_kernel(qt_ref, kt_ref, q_ref, k_ref, vt_ref, fq_ref, fk_ref, o_ref,
                 qm_ref, m_ref, acc_ref, *, tq, tk, hd):
    t = pl.program_id(2)
    qi, ki = qt_ref[t], kt_ref[t]
    hpp = LANE // hd
    ones = jnp.ones((ONES_ROWS, tk), BF16)

    @pl.when(ki == 0)
    def _():
        lane = lax.broadcasted_iota(jnp.int32, (1, LANE), 1)
        q = q_ref[...]
        for h in range(hpp):
            qm_ref[h] = jnp.where((lane >= hd * h) & (lane < hd * (h + 1)), q, 0)
        m_ref[...] = jnp.full_like(m_ref, NEG)
        acc_ref[...] = jnp.zeros_like(acc_ref)

    def body(masked):
        kk = k_ref[...]
        if masked:
            kpos = ki * tk + lax.broadcasted_iota(jnp.int32, (tk, tq), 0)
            qpos = qi * tq + lax.broadcasted_iota(jnp.int32, (tk, tq), 1)
            visible = kpos <= qpos
        for h in range(hpp):
            u = (lax.dot_general(kk, qm_ref[h], NT, preferred_element_type=F32)
                 - fk_ref[:, h:h + 1])
            if masked:
                u = jnp.where(visible, u, NEG)
            fq = fq_ref[h:h + 1, :]
            m_prev = m_ref[h]
            m_new = jnp.maximum(m_prev, jnp.max(u, axis=0, keepdims=True) + fq)
            alpha = jnp.exp2(m_prev - m_new)
            p = jnp.exp2(u + (fq - m_new)).astype(BF16)
            m_ref[h] = m_new
            v_one = jnp.concatenate([vt_ref[h * hd:(h + 1) * hd, :], ones], axis=0)
            acc_ref[h] = alpha * acc_ref[h] + jnp.dot(v_one, p, preferred_element_type=F32)

    needs_mask = (ki + 1) * tk - 1 > qi * tq

    @pl.when(needs_mask)
    def _():
        body(True)

    @pl.when(jnp.logical_not(needs_mask))
    def _():
        body(False)

    @pl.when((ki + 1) * tk >= (qi + 1) * tq)
    def _():
        o = jnp.concatenate([acc_ref[h, 0:hd] * (1.0 / acc_ref[h, hd:hd + 1]) for h in range(hpp)],
                            axis=0)
        o_ref[...] = o.T.astype(BF16)


def _attn_prompt(qkv, vt, fq, fk, b, t, fw, hd, tq, tk):
    npair = fw // LANE
    nq, nkb = t // tq, t // tk
    pairs = [(i, j) for i in range(nq) for j in range(((i + 1) * tq - 1) // tk + 1)]
    qt = jnp.asarray(np.array([p[0] for p in pairs], np.int32))
    kt = jnp.asarray(np.array([p[1] for p in pairs], np.int32))
    hpp = LANE // hd
    grid_spec = pltpu.PrefetchScalarGridSpec(
        num_scalar_prefetch=2,
        grid=(b, npair, len(pairs)),
        in_specs=[pl.BlockSpec((tq, LANE), lambda bi, p, s, qt, kt: (bi * nq + qt[s], p)),
                  pl.BlockSpec((tk, LANE), lambda bi, p, s, qt, kt: (bi * nkb + kt[s], npair + p)),
                  pl.BlockSpec((LANE, tk), lambda bi, p, s, qt, kt: (p, bi * nkb + kt[s])),
                  pl.BlockSpec((None, None, hpp, tq), lambda bi, p, s, qt, kt: (bi, p, 0, qt[s])),
                  pl.BlockSpec((None, None, tk, hpp), lambda bi, p, s, qt, kt: (bi, p, kt[s], 0))],
        out_specs=pl.BlockSpec((tq, LANE), lambda bi, p, s, qt, kt: (bi * nq + qt[s], p)),
        scratch_shapes=[pltpu.VMEM((hpp, tq, LANE), BF16),
                        pltpu.VMEM((hpp, 1, tq), F32),
                        pltpu.VMEM((hpp, hd + ONES_ROWS, tq), F32)],
    )
    return pl.pallas_call(
        functools.partial(_attn_kernel, tq=tq, tk=tk, hd=hd),
        grid_spec=grid_spec,
        out_shape=jax.ShapeDtypeStruct((b * t, fw), BF16),
        compiler_params=_cparams(("arbitrary", "arbitrary", "arbitrary")),
        name="fox_attn_prompt",
    )(qt, kt, qkv, qkv, vt, fq, fk)


def _pair_layouts(f2, b, fh, hd):
    t = f2.shape[0]
    hpp = LANE // hd
    f4 = f2.reshape(t, b, fh // hpp, hpp)
    return f4.transpose(1, 2, 3, 0), f4.transpose(1, 2, 0, 3)


def _attn_cache_kernel(qkv_ref, kc_ref, vc_ref, fq_ref, fkc_ref, fkn_ref, o_ref, m_ref, l_ref, acc_ref,
                       *, fh, hd, tk, nkb):
    kb = pl.program_id(1)
    fw = fh * hd
    s_len = qkv_ref.shape[0]

    @pl.when(kb == 0)
    def _():
        m_ref[...] = jnp.full_like(m_ref, NEG)
        l_ref[...] = jnp.zeros_like(l_ref)
        acc_ref[...] = jnp.zeros_like(acc_ref)

    def update(key_of, val_of, fk_ref, mask):
        for h0 in range(0, fh, CACHE_HEAD_GROUP):
            heads = range(h0, min(h0 + CACHE_HEAD_GROUP, fh))
            rows = slice(h0 * s_len, (h0 + len(heads)) * s_len)
            s = jnp.concatenate(
                [lax.dot_general(qkv_ref[:, h * hd:(h + 1) * hd], key_of(h), NT, preferred_element_type=F32)
                 + (fq_ref[:, h:h + 1] - fk_ref[h:h + 1, :]) for h in heads], axis=0)
            if mask is not None:
                s = jnp.where(mask, s, NEG)
            m_prev = m_ref[rows]
            m_new = jnp.maximum(m_prev, jnp.max(s, axis=-1, keepdims=True))
            alpha = jnp.exp2(m_prev - m_new)
            p = jnp.exp2(s - m_new)
            l_ref[rows] = alpha * l_ref[rows] + jnp.sum(p, axis=-1, keepdims=True)
            m_ref[rows] = m_new
            pb = p.astype(BF16)
            pv = jnp.concatenate(
                [jnp.dot(pb[i * s_len:(i + 1) * s_len], val_of(h), preferred_element_type=F32)
                 for i, h in enumerate(heads)], axis=0)
            acc_ref[rows] = alpha * acc_ref[rows] + pv

    kc2 = kc_ref if len(kc_ref.shape) == 2 else kc_ref.reshape(tk * fh, hd)
    vc2 = vc_ref if len(vc_ref.shape) == 2 else vc_ref.reshape(tk * fh, hd)
    update(lambda h: kc2[pl.ds(h, tk, stride=fh), :].astype(BF16),
           lambda h: vc2[pl.ds(h, tk, stride=fh), :].astype(BF16), fkc_ref, None)

    @pl.when(kb == nkb - 1)
    def _():
        grp = min(CACHE_HEAD_GROUP, fh) * s_len
        row = lax.broadcasted_iota(jnp.int32, (grp, s_len), 0)
        col = lax.broadcasted_iota(jnp.int32, (grp, s_len), 1)
        update(lambda h: qkv_ref[:, fw + h * hd:fw + (h + 1) * hd],
               lambda h: qkv_ref[:, 2 * fw + h * hd:2 * fw + (h + 1) * hd], fkn_ref,
               col <= row % s_len)
        o = acc_ref[...] * (1.0 / l_ref[...])
        o_ref[...] = jnp.concatenate([o[h * s_len:(h + 1) * s_len] for h in range(fh)], axis=1).astype(BF16)


def _attn_cached(qkv, cache_k, cache_v, fq, fkc, fkn, l, b, s_len, fh, hd):
    fw = fh * hd
    past = fkc.shape[2]
    tk = _pick(past, 512, 8)
    nkb = past // tk
    if cache_k.ndim == 5:
        cspec = pl.BlockSpec((None, None, tk, fh, hd), lambda bi, kb: (l, bi, kb, 0, 0))
    else:
        cspec = pl.BlockSpec((None, None, tk * fh, hd), lambda bi, kb: (l, bi, kb, 0))
    return pl.pallas_call(
        functools.partial(_attn_cache_kernel, fh=fh, hd=hd, tk=tk, nkb=nkb),
        grid=(b, nkb),
        in_specs=[pl.BlockSpec((s_len, 3 * fw), lambda bi, kb: (bi, 0)),
                  cspec, cspec,
                  pl.BlockSpec((None, s_len, fh), lambda bi, kb: (bi, 0, 0)),
                  pl.BlockSpec((None, fh, tk), lambda bi, kb: (bi, 0, kb)),
                  pl.BlockSpec((None, fh, s_len), lambda bi, kb: (bi, 0, 0))],
        out_specs=pl.BlockSpec((s_len, fw), lambda bi, kb: (bi, 0)),
        out_shape=jax.ShapeDtypeStruct((b * s_len, fw), BF16),
        scratch_shapes=[pltpu.VMEM((fh * s_len, 1), F32), pltpu.VMEM((fh * s_len, 1), F32),
                        pltpu.VMEM((fh * s_len, hd), F32)],
        compiler_params=_cparams(("arbitrary", "arbitrary")),
        name="fox_attn_cached",
    )(qkv, cache_k, cache_v, fq, fkc, fkn)


def _hgrn_kernel(q_ref, f_ref, i_ref, g_ref, lb_ref, gn_ref, s0_ref, o_ref, sfin_ref, st_ref,
                 *, c, nc, hh, hk):
    ci = pl.program_id(1)

    @pl.when(ci == 0)
    def _():
        for h in range(hh):
            st_ref[h] = s0_ref[h].astype(F32).T

    lb = lb_ref[...]
    f = lb + (1.0 - lb) * _sigmoid(f_ref[...])
    k_all = 1.0 - f
    g = jnp.log(f)
    cl = g.shape[0]
    hw = hh * hk

    b3 = jnp.dot(_lower_tri(cl), jnp.concatenate(_split3(g), axis=1), preferred_element_type=F32)
    b_all = b3[:, :hw] + b3[:, hw:2 * hw] + b3[:, 2 * hw:]
    q_all, v_all, gate_all = q_ref[...], i_ref[...], g_ref[...]

    sub_s = lax.broadcasted_iota(jnp.int32, (c, c, c), 0)
    sub_t = lax.broadcasted_iota(jnp.int32, (c, c, c), 1)
    sub_n = lax.broadcasted_iota(jnp.int32, (c, c, c), 2)
    pick = (sub_n == sub_s) & (sub_t >= sub_s)
    ys = []
    for h in range(hh):
        sl = slice(h * hk, (h + 1) * hk)
        q, k, v, b = q_all[:, sl], k_all[:, sl], v_all[:, sl], b_all[:, sl]
        vb = v.astype(BF16)
        st = st_ref[h]
        o_inter = lax.dot_general((q * jnp.exp(b)).astype(BF16), st.astype(BF16), NT,
                                  preferred_element_type=F32)
        outs = []
        for i in range(cl // c):
            lo_r, hi_r = i * c, (i + 1) * c
            qi, bi = q[lo_r:hi_r], b[lo_r:hi_r]
            x = jnp.concatenate(
                [qi * jnp.exp(jnp.minimum(bi - b[lo_r + s:lo_r + s + 1], 0.0)) for s in range(c)], axis=0)
            r = lax.dot_general(x.astype(BF16), k[lo_r:hi_r].astype(BF16), NT, preferred_element_type=F32)
            a_diag = jnp.sum(jnp.where(pick, r.reshape(c, c, c), 0.0), axis=0)
            oi = jnp.dot(a_diag.astype(BF16), vb[lo_r:hi_r], preferred_element_type=F32)
            if i > 0:
                ref = b[lo_r - 1:lo_r]
                qd = qi * jnp.exp(bi - ref)
                kdec = k[:lo_r] * jnp.exp(ref - b[:lo_r])
                a_off = lax.dot_general(qd.astype(BF16), kdec.astype(BF16), NT, preferred_element_type=F32)
                oi = oi + jnp.dot(a_off.astype(BF16), vb[:lo_r], preferred_element_type=F32)
            outs.append(oi)
        o = o_inter + jnp.concatenate(outs, axis=0)

        b_last = b[cl - 1:cl]
        k_end = (k * jnp.exp(b_last - b)).astype(BF16)
        st_ref[h] = st * jnp.exp(b_last) + lax.dot_general(vb, k_end, TN, preferred_element_type=F32)

        y = o * lax.rsqrt(jnp.mean(o * o, axis=-1, keepdims=True) + EPS) * gn_ref[...]
        gate = gate_all[:, sl]
        ys.append(y * (gate * _sigmoid(gate)))
    o_ref[...] = jnp.concatenate(ys, axis=1).astype(BF16)

    @pl.when(ci == nc - 1)
    def _():
        for h in range(hh):
            sfin_ref[h] = st_ref[h].T


def _hgrn(proj, lb, gn, s0, b, t, col0, hh, hk):
    cl = _pick(t, 64, HGRN_SUB)
    nc = t // cl
    hw = hh * hk

    def col(kind):
        return pl.BlockSpec((cl, hw), lambda bi, ci: (bi * nc + ci, col0 + kind))

    return pl.pallas_call(
        functools.partial(_hgrn_kernel, c=HGRN_SUB, nc=nc, hh=hh, hk=hk),
        grid=(b, nc),
        in_specs=[col(0), col(1), col(2), col(3),
                  pl.BlockSpec((1, hw), lambda bi, ci: (0, 0)),
                  pl.BlockSpec((1, hk), lambda bi, ci: (0, 0)),
                  pl.BlockSpec((None, hh, hk, hk), lambda bi, ci: (bi, 0, 0, 0))],
        out_specs=[pl.BlockSpec((cl, hw), lambda bi, ci: (bi * nc + ci, 0)),
                   pl.BlockSpec((None, hh, hk, hk), lambda bi, ci: (bi, 0, 0, 0))],
        out_shape=[jax.ShapeDtypeStruct((b * t, hw), BF16),
                   jax.ShapeDtypeStruct((b, hh, hk, hk), F32)],
        scratch_shapes=[pltpu.VMEM((hh, hk, hk), F32)],
        compiler_params=_cparams(("arbitrary", "arbitrary")),
        name="hgrn2",
    )(proj, proj, proj, proj, lb, gn, s0)


def _conv_kernel(ga_ref, gb_ref, buf_ref, w_ref, b_ref, lg_ref, lb_ref, o_ref, nb_ref, xs_ref,
                 *, tb, nblk, width, halo, rb):
    t = pl.program_id(1)
    pad = halo - (width - 1)

    @pl.when(t == 0)
    def _():
        xs_ref[0:halo, :] = jnp.zeros((halo, xs_ref.shape[1]), F32)
        xs_ref[pad:halo, :] = buf_ref[...].astype(F32)

    @pl.when(t > 0)
    def _():
        xs_ref[0:halo, :] = xs_ref[tb:tb + halo, :]

    xs_ref[halo:halo + tb, :] = ga_ref[...] * _sigmoid(gb_ref[...])
    for r0 in range(0, tb, rb):
        acc = xs_ref[r0 + pad:r0 + pad + rb, :] * w_ref[0:1, :]
        for w in range(1, width):
            acc = acc + xs_ref[r0 + pad + w:r0 + pad + w + rb, :] * w_ref[w:w + 1, :]
        y = acc + b_ref[...]
        mu = jnp.mean(y, axis=-1, keepdims=True)
        yc = y - mu
        var = jnp.mean(yc * yc, axis=-1, keepdims=True)
        z = yc * lax.rsqrt(var + EPS) * lg_ref[...] + lb_ref[...]
        o_ref[r0:r0 + rb, :] = (z * _sigmoid(z)).astype(BF16)

    @pl.when(t == nblk - 1)
    def _():
        nb_ref[...] = xs_ref[tb + pad:tb + halo, :]


def _conv(proj, buf, dw_w, dw_b, ln_g, ln_b, b, t, ga_blk, cc):
    width = dw_w.shape[0]
    halo = -(-(width - 1) // 8) * 8
    tb = _pick(t, 256, 8)
    nblk = t // tb
    rb = _pick(tb, 64, 8)
    vec = pl.BlockSpec((1, cc), lambda bi, j: (0, 0))
    return pl.pallas_call(
        functools.partial(_conv_kernel, tb=tb, nblk=nblk, width=width, halo=halo, rb=rb),
        grid=(b, nblk),
        in_specs=[pl.BlockSpec((tb, cc), lambda bi, j: (bi * nblk + j, ga_blk)),
                  pl.BlockSpec((tb, cc), lambda bi, j: (bi * nblk + j, ga_blk + 1)),
                  pl.BlockSpec((None, width - 1, cc), lambda bi, j: (bi, 0, 0)),
                  pl.BlockSpec((width, cc), lambda bi, j: (0, 0)),
                  vec, vec, vec],
        out_specs=[pl.BlockSpec((tb, cc), lambda bi, j: (bi * nblk + j, 0)),
                   pl.BlockSpec((None, width - 1, cc), lambda bi, j: (bi, 0, 0))],
        out_shape=[jax.ShapeDtypeStruct((b * t, cc), BF16),
                   jax.ShapeDtypeStruct((b, width - 1, cc), F32)],
        scratch_shapes=[pltpu.VMEM((halo + tb, cc), F32)],
        compiler_params=_cparams(("arbitrary", "arbitrary")),
        name="conv_module",
    )(proj, proj, buf, dw_w, dw_b, ln_g, ln_b)


def _outproj_kernel(x_ref, gt_ref, fox_ref, hg_ref, cv_ref, w1_ref, w2_ref, w3_ref, o_ref):
    acc = (jnp.dot(fox_ref[...], w1_ref[...].astype(BF16), preferred_element_type=F32)
           + jnp.dot(hg_ref[...], w2_ref[...].astype(BF16), preferred_element_type=F32)
           + jnp.dot(cv_ref[...], w3_ref[...].astype(BF16), preferred_element_type=F32))
    o_ref[...] = _gated_residual(x_ref[...], gt_ref[...], acc, 1.0)


def _outproj(x, mod6, fox, hg, cv, w_out, l, s, r0, seq_len, tm):
    m, d = x.shape
    fw, hw, cc = fox.shape[1], hg.shape[1], cv.shape[1]
    tn = _pick(d, 512, LANE)
    return pl.pallas_call(
        _outproj_kernel,
        grid=(m // tm, d // tn),
        in_specs=[pl.BlockSpec((tm, tn), lambda i, j: (i, j)),
                  _mod_spec(tm, seq_len, tn, (l, s, 2, r0), tiled=True),
                  pl.BlockSpec((tm, fw), lambda i, j: (i, 0)),
                  pl.BlockSpec((tm, hw), lambda i, j: (i, 0)),
                  pl.BlockSpec((tm, cc), lambda i, j: (i, 0)),
                  pl.BlockSpec((None, fw, tn), lambda i, j: (l, 0, j)),
                  pl.BlockSpec((None, hw, tn), lambda i, j: (l, fw // hw, j)),
                  pl.BlockSpec((None, cc, tn), lambda i, j: (l, (fw + hw) // cc, j))],
        out_specs=pl.BlockSpec((tm, tn), lambda i, j: (i, j)),
        out_shape=jax.ShapeDtypeStruct((m, d), F32),
        compiler_params=_cparams(("arbitrary", "arbitrary")),
        name="mix_outproj",
    )(x, mod6, fox, hg, cv, w_out, w_out, w_out)


def _norm_kernel(x_ref, g_ref, o_ref):
    x = x_ref[...]
    o_ref[...] = x * lax.rsqrt(jnp.mean(x * x, axis=-1, keepdims=True) + EPS) * g_ref[...]


def _final_norm(x, g, tm):
    m, d = x.shape
    return pl.pallas_call(
        _norm_kernel,
        grid=(m // tm,),
        in_specs=[pl.BlockSpec((tm, d), lambda i: (i, 0)), pl.BlockSpec((1, d), lambda i: (0, 0))],
        out_specs=pl.BlockSpec((tm, d), lambda i: (i, 0)),
        out_shape=jax.ShapeDtypeStruct((m, d), F32),
        compiler_params=_cparams(("arbitrary",)),
        name="final_norm",
    )(x, g)


def _time_major(a, b, t):
    h = a.shape[-1]
    return a.reshape(b, t, h).transpose(1, 0, 2).reshape(t, b * h)


def _batch_major(a, b):
    t = a.shape[0]
    return a.reshape(t, b, -1).transpose(1, 0, 2)


def kernel(x_prompt, x_sample, cache_fox_k, cache_fox_v, cache_fox_logf, state_hgrn, state_conv,
           c_prompt, c_sample, norm_g, ada_w, ada_b, ffn_w_in, ffn_w_out, mix_w_in, mix_w_out,
           fox_f_bias, hgrn_lb_logits, hgrn_norm_g, conv_dw_w, conv_dw_b, conv_ln_g, conv_ln_b,
           final_norm_g):
    bp, seq, d = x_prompt.shape
    bs, dseq, _ = x_sample.shape
    depth = norm_g.shape[0]
    nsub = norm_g.shape[1]
    past = cache_fox_k.shape[2]
    fh, hd = cache_fox_k.shape[3], cache_fox_k.shape[4]
    fw = fh * hd
    hh, hk = state_hgrn.shape[2], state_hgrn.shape[3]
    hw = hh * hk
    cc = conv_dw_b.shape[1]
    width = conv_dw_w.shape[1]
    qscale = float(hd) ** -0.5 * LOG2E

    lb_sm = jnp.cumsum(jax.nn.softmax(hgrn_lb_logits.astype(F32), axis=0), axis=0)
    lb_all = lb_sm - lb_sm[0:1]

    row0_sample = -(-bp // 16) * 16
    c_rows = jnp.concatenate([jnp.pad(c_prompt, ((0, row0_sample - bp), (0, 0))),
                              jnp.pad(c_sample, ((0, -bs % 16), (0, 0)))], axis=0)
    mod6 = _ada(jax.nn.silu(c_rows).astype(BF16), ada_w, ada_b).reshape(depth, -1, nsub, 3, 1, d)
    g3 = norm_g.reshape(depth * nsub, 1, d)

    tn_in = _pick(fw, 512, LANE)
    n_rest = mix_w_in.shape[2] - 3 * fw - fh
    assert n_rest % tn_in == 0 and fh <= tn_in
    w_rest = jnp.concatenate(
        [mix_w_in[:, :, 3 * fw + fh:], mix_w_in[:, :, 3 * fw:3 * fw + fh],
         jnp.zeros((depth, d, tn_in - fh), F32)], axis=2)
    fz0 = 3 * fw + n_rest
    blk_hgrn = 3 * fw // hw
    blk_conv = (3 * fw + 4 * hw) // cc
    ffn_bf16 = {}

    def trunk(x, b, t, row0, cached):
        m = b * t
        tm = _pick(m, 1024, 8) if t >= 1024 else _pick(m, 512, t)
        tm_ffn = _pick(m, FFN_TM, 8) if t >= FFN_TM else tm
        tq = _pick(t, 1024, LANE)
        bias_row = lambda l: jnp.tile(fox_f_bias[l], b)[None]

        def ffn(x, l, s):
            if cached:
                x, *ffn_bf16[l, s] = _ffn(x, g3, mod6, ffn_w_in, ffn_w_in, ffn_w_out, l, s, row0, t,
                                          tm_ffn, True)
                return x
            return _ffn(x, g3, mod6, *ffn_bf16[l, s], l, s, row0, t, tm_ffn, False)[0]

        new = []
        for l in range(depth):
            x = ffn(x, l, 0)
            proj, qkv = _inproj(x, g3, mod6, mix_w_in, w_rest, l, 1, row0, fw, tn_in, qscale, t, tm)
            k_new = proj[:, fw:2 * fw].reshape(b, t, fh, hd)
            v_new = proj[:, 2 * fw:3 * fw].reshape(b, t, fh, hd)
            logf_tm, fcum = _logf_cumsum(_time_major(proj[:, fz0:fz0 + fh], b, t), bias_row(l), True)
            logf = _batch_major(logf_tm, b)
            if not cached:
                fq, fk = _pair_layouts(fcum * LOG2E, b, fh, hd)
                fox = _attn_prompt(qkv, qkv[:, 2 * fw:].T, fq, fk, b, t, fw, hd, tq, tq)
                s0 = jnp.zeros((b, hh, hk, hk), F32)
                buf = jnp.zeros((b, width - 1, cc), F32)
            else:
                cl_tm = cache_fox_logf[l].astype(F32).transpose(1, 0, 2).reshape(past, b * fh)
                _, fc = _logf_cumsum(cl_tm, jnp.zeros((1, b * fh), F32), False)
                f2_new = _batch_major((fcum + fc[-1:]) * LOG2E, b)
                f2_old = _batch_major(fc * LOG2E, b)
                fox = _attn_cached(qkv, cache_fox_k, cache_fox_v, f2_new, f2_old.transpose(0, 2, 1),
                                   f2_new.transpose(0, 2, 1), l, b, t, fh, hd)
                s0 = state_hgrn[l]
                buf = state_conv[l]
            hg, s_fin = _hgrn(proj, lb_all[l][None], hgrn_norm_g[l][None], s0, b, t, blk_hgrn, hh, hk)
            cv, new_buf = _conv(proj, buf, conv_dw_w[l], conv_dw_b[l][None], conv_ln_g[l][None],
                                conv_ln_b[l][None], b, t, blk_conv, cc)
            x = _outproj(x, mod6, fox, hg, cv, mix_w_out, l, 1, row0, t, tm)
            x = ffn(x, l, 2)
            new.append((k_new, v_new, logf, s_fin, new_buf))
        y = _final_norm(x, final_norm_g[None], tm).reshape(b, t, d)
        return y, [jnp.stack([s[i] for s in new]) for i in range(5)]

    y_s, (ks, vs, fs, hs, cs) = trunk(x_sample.reshape(bs * dseq, d), bs, dseq, row0_sample, True)
    y_p, (kp, vp, fp, hp, cp) = trunk(x_prompt.reshape(bp * seq, d), bp, seq, 0, False)
    return (y_p, y_s, kp, vp, fp, hp, cp, ks, vs, fs, hs, cs)
```

```python
import functools

import jax
import jax.numpy as jnp
import numpy as np
from jax import lax
from jax.experimental import pallas as pl
from jax.experimental.pallas import tpu as pltpu

F32 = jnp.float32
BF16 = jnp.bfloat16
EPS = 1e-6
NEG = -1e30
LOG2E = 1.4426950408889634
LANE = 128
ONES_ROWS = 16
CACHE_HEAD_GROUP = 4
HGRN_SUB = 16
FFN_TM, FFN_TF = 512, 512
FFN_TF_EMIT = 256
VMEM_LIMIT = 60 * 1024 * 1024

NT = (((1,), (1,)), ((), ()))
TN = (((0,), (0,)), ((), ()))


def _pick(n, target, mult=8):
    for t in range(min(n, target), 0, -1):
        if n % t == 0 and t % mult == 0:
            return t
    return n


def _cparams(sem):
    return pltpu.CompilerParams(dimension_semantics=sem, vmem_limit_bytes=VMEM_LIMIT)


def _sigmoid(x):
    return 1.0 / (1.0 + jnp.exp(-x))


def _split3(x):
    hi = x.astype(BF16)
    r1 = x - hi.astype(F32)
    mid = r1.astype(BF16)
    lo = (r1 - mid.astype(F32)).astype(BF16)
    return hi, mid, lo


def _lower_tri(n):
    return (lax.broadcasted_iota(jnp.int32, (n, n), 0)
            >= lax.broadcasted_iota(jnp.int32, (n, n), 1)).astype(BF16)


def _modulate(x, g, shift, scale):
    tm, d = x.shape
    grp = shift.shape[0]
    y = x * lax.rsqrt(jnp.mean(x * x, axis=-1, keepdims=True) + EPS) * g
    y = y.reshape(grp, tm // grp, d) * (1.0 + scale) + shift
    return y.reshape(tm, d)


def _gated_residual(x, gate, upd, coef):
    tm, d = x.shape
    grp = gate.shape[0]
    y = x.reshape(grp, tm // grp, d) + (coef * gate) * upd.reshape(grp, tm // grp, d)
    return y.reshape(tm, d)


def _mod_spec(tm, seq_len, tn, where, tiled=False):
    l, s, kind, r0 = where
    grp = max(1, tm // seq_len)
    assert r0 % grp == 0
    return pl.BlockSpec(
        (None, grp, None, None, 1, tn),
        lambda i, j: (l, r0 // grp + (i * tm) // (seq_len * grp), s, kind, 0, j if tiled else 0))


def _vec_spec(n, idx):
    return pl.BlockSpec((None, 1, n), lambda *_: (idx, 0, 0))


def _ada_kernel(c_ref, w_ref, b_ref, o_ref):
    w = w_ref[...].astype(BF16)
    o_ref[...] = jnp.dot(c_ref[...], w, preferred_element_type=F32) + b_ref[...]


def _ada(c_act, ada_w, ada_b):
    depth, d, n = ada_w.shape
    r = c_act.shape[0]
    tn = _pick(n, 1024, LANE)
    return pl.pallas_call(
        _ada_kernel,
        grid=(depth, n // tn),
        in_specs=[pl.BlockSpec((r, d), lambda l, j: (0, 0)),
                  pl.BlockSpec((None, d, tn), lambda l, j: (l, 0, j)),
                  pl.BlockSpec((None, 1, tn), lambda l, j: (l, 0, j))],
        out_specs=pl.BlockSpec((None, r, tn), lambda l, j: (l, 0, j)),
        out_shape=jax.ShapeDtypeStruct((depth, r, n), F32),
        compiler_params=_cparams(("arbitrary", "arbitrary")),
        name="ada_proj",
    )(c_act, ada_w, ada_b.reshape(depth, 1, n))


def _ffn_kernel(x_ref, g_ref, sh_ref, sc_ref, gt_ref, wa_ref, wb_ref, wo_ref, o_ref, *rest, nj, emit):
    if emit:
        wab_ref, wbb_ref, wob_ref, h_ref = rest
    else:
        (h_ref,) = rest
    j = pl.program_id(1)

    @pl.when(j == 0)
    def _():
        h_ref[...] = _modulate(x_ref[...], g_ref[...], sh_ref[...], sc_ref[...]).astype(BF16)
        o_ref[...] = jnp.zeros_like(o_ref)

    wa, wb, wo = wa_ref[...], wb_ref[...], wo_ref[...]
    if emit:
        wa, wb, wo = wa.astype(BF16), wb.astype(BF16), wo.astype(BF16)
        wab_ref[...] = wa
        wbb_ref[...] = wb
        wob_ref[...] = wo
    h = h_ref[...]
    a = jnp.dot(h, wa, preferred_element_type=F32)
    b = jnp.dot(h, wb, preferred_element_type=F32)
    act = (a * _sigmoid(a) * b).astype(BF16)
    o_ref[...] += jnp.dot(act, wo, preferred_element_type=F32)

    @pl.when(j == nj - 1)
    def _():
        o_ref[...] = _gated_residual(x_ref[...], gt_ref[...], o_ref[...], 0.5)


def _ffn(x, g3, mod6, wa, wb, wo, l, s, r0, seq_len, tm, emit):
    m, d = x.shape
    dff = wo.shape[-2]
    tf = _pick(dff, FFN_TF_EMIT if emit else FFN_TF, LANE)
    nj = dff // tf
    nsub = mod6.shape[2]
    mods = [_mod_spec(tm, seq_len, d, (l, s, kind, r0)) for kind in range(3)]
    if emit:
        assert m == tm
        w_specs = [pl.BlockSpec((None, None, d, tf), lambda i, j: (l, s // 2, 0, j)),
                   pl.BlockSpec((None, None, d, tf), lambda i, j: (l, s // 2, 0, j + nj)),
                   pl.BlockSpec((None, None, tf, d), lambda i, j: (l, s // 2, j, 0))]
    bf_specs = [pl.BlockSpec((d, tf), lambda i, j: (0, j)),
                pl.BlockSpec((d, tf), lambda i, j: (0, j)),
                pl.BlockSpec((tf, d), lambda i, j: (j, 0))]
    if not emit:
        w_specs = bf_specs
    out_specs = [pl.BlockSpec((tm, d), lambda i, j: (i, 0))]
    out_shape = [jax.ShapeDtypeStruct((m, d), F32)]
    if emit:
        out_specs += bf_specs
        out_shape += [jax.ShapeDtypeStruct((d, dff), BF16)] * 2 + [jax.ShapeDtypeStruct((dff, d), BF16)]
    return pl.pallas_call(
        functools.partial(_ffn_kernel, nj=nj, emit=emit),
        grid=(m // tm, nj),
        in_specs=[pl.BlockSpec((tm, d), lambda i, j: (i, 0)), _vec_spec(d, l * nsub + s)] + mods + w_specs,
        out_specs=out_specs,
        out_shape=out_shape,
        scratch_shapes=[pltpu.VMEM((tm, d), BF16)],
        compiler_params=_cparams(("arbitrary", "arbitrary")),
        name="ffn",
    )(x, g3, mod6, mod6, mod6, wa, wb, wo)


def _inproj_kernel(x_ref, g_ref, sh_ref, sc_ref, wq_ref, wr_ref, proj_ref, qkv_ref, h_ref,
                   *, nq, nqkv, qscale):
    j = pl.program_id(1)

    @pl.when(j == 0)
    def _():
        h_ref[...] = _modulate(x_ref[...], g_ref[...], sh_ref[...], sc_ref[...]).astype(BF16)

    @pl.when(j < nqkv)
    def _():
        r = jnp.dot(h_ref[...], wq_ref[...].astype(BF16), preferred_element_type=F32)
        proj_ref[...] = r
        qkv_ref[...] = (r * jnp.where(j < nq, qscale, 1.0)).astype(BF16)

    @pl.when(j >= nqkv)
    def _():
        proj_ref[...] = jnp.dot(h_ref[...], wr_ref[...].astype(BF16), preferred_element_type=F32)


def _inproj(x, g3, mod6, w_in, w_rest, l, s, r0, fw, tn, qscale, seq_len, tm):
    m, d = x.shape
    nq, nqkv = fw // tn, 3 * fw // tn
    n_pad = 3 * fw + w_rest.shape[2]
    nsub = mod6.shape[2]
    return pl.pallas_call(
        functools.partial(_inproj_kernel, nq=nq, nqkv=nqkv, qscale=qscale),
        grid=(m // tm, n_pad // tn),
        in_specs=[pl.BlockSpec((tm, d), lambda i, j: (i, 0)), _vec_spec(d, l * nsub + s),
                  _mod_spec(tm, seq_len, d, (l, s, 0, r0)), _mod_spec(tm, seq_len, d, (l, s, 1, r0)),
                  pl.BlockSpec((None, d, tn), lambda i, j: (l, 0, jnp.minimum(j, nqkv - 1))),
                  pl.BlockSpec((None, d, tn), lambda i, j: (l, 0, jnp.maximum(j - nqkv, 0)))],
        out_specs=[pl.BlockSpec((tm, tn), lambda i, j: (i, j)),
                   pl.BlockSpec((tm, tn), lambda i, j: (i, jnp.minimum(j, nqkv - 1)))],
        out_shape=[jax.ShapeDtypeStruct((m, n_pad), F32),
                   jax.ShapeDtypeStruct((m, 3 * fw), BF16)],
        scratch_shapes=[pltpu.VMEM((tm, d), BF16)],
        compiler_params=_cparams(("arbitrary", "arbitrary")),
        name="mix_inproj",
    )(x, g3, mod6, mod6, w_in, w_rest)


def _cumsum_kernel(z_ref, bias_ref, lf_ref, f_ref, carry_ref, *, logsig):
    t = pl.program_id(0)

    @pl.when(t == 0)
    def _():
        carry_ref[...] = jnp.zeros_like(carry_ref)

    z = z_ref[...]
    if logsig:
        z = z + bias_ref[...]
        z = jnp.minimum(z, 0.0) - jnp.log(1.0 + jnp.exp(-jnp.abs(z)))
    lf_ref[...] = z
    tb = z.shape[0]
    tri = _lower_tri(tb)
    hi, mid, lo = _split3(z)
    cs = (jnp.dot(tri, hi, preferred_element_type=F32) + jnp.dot(tri, mid, preferred_element_type=F32)
          + jnp.dot(tri, lo, preferred_element_type=F32))
    f = cs + carry_ref[...]
    f_ref[...] = f
    carry_ref[...] = f[tb - 1:tb, :]


def _logf_cumsum(z, bias, logsig):
    t, n = z.shape
    tb = _pick(t, 512, 8)
    spec = pl.BlockSpec((tb, n), lambda i: (i, 0))
    return pl.pallas_call(
        functools.partial(_cumsum_kernel, logsig=logsig),
        grid=(t // tb,),
        in_specs=[spec, pl.BlockSpec((1, n), lambda i: (0, 0))],
        out_specs=[spec, spec],
        out_shape=[jax.ShapeDtypeStruct((t, n), F32)] * 2,
        scratch_shapes=[pltpu.VMEM((1, n), F32)],
        compiler_params=_cparams(("arbitrary",)),
        name="logf_cumsum",
    )(z, bias)


def _attn_kernel(qt_ref, kt_ref, q_ref, k_ref, vt_ref, fq_ref, fk_ref, o_ref,
                 qm_ref, m_ref, acc_ref, *, tq, tk, hd):
    t = pl.program_id(2)
    qi, ki = qt_ref[t], kt_ref[t]
    hpp = LANE // hd
    ones = jnp.ones((ONES_ROWS, tk), BF16)

    @pl.when(ki == 0)
    def _():
        lane = lax.broadcasted_iota(jnp.int32, (1, LANE), 1)
        q = q_ref[...]
        for h in range(hpp):
            qm_ref[h] = jnp.where((lane >= hd * h) & (lane < hd * (h + 1)), q, 0)
        m_ref[...] = jnp.full_like(m_ref, NEG)
        acc_ref[...] = jnp.zeros_like(acc_ref)

    def body(masked):
        kk = k_ref[...]
        if masked:
            kpos = ki * tk + lax.broadcasted_iota(jnp.int32, (tk, tq), 0)
            qpos = qi * tq + lax.broadcasted_iota(jnp.int32, (tk, tq), 1)
            visible = kpos <= qpos
        for h in range(hpp):
            u = (lax.dot_general(kk, qm_ref[h], NT, preferred_element_type=F32)
                 - fk_ref[:, h:h + 1])
            if masked:
                u = jnp.where(visible, u, NEG)
            fq = fq_ref[h:h + 1, :]
            m_prev = m_ref[h]
            m_new = jnp.maximum(m_prev, jnp.max(u, axis=0, keepdims=True) + fq)
            alpha = jnp.exp2(m_prev - m_new)
            p = jnp.exp2(u + (fq - m_new)).astype(BF16)
            m_ref[h] = m_new
            v_one = jnp.concatenate([vt_ref[h * hd:(h + 1) * hd, :], ones], axis=0)
            acc_ref[h] = alpha * acc_ref[h] + jnp.dot(v_one, p, preferred_element_type=F32)

    needs_mask = (ki + 1) * tk - 1 > qi * tq

    @pl.when(needs_mask)
    def _():
        body(True)

    @pl.when(jnp.logical_not(needs_mask))
    def _():
        body(False)

    @pl.when((ki + 1) * tk >= (qi + 1) * tq)
    def _():
        o = jnp.concatenate([acc_ref[h, 0:hd] * (1.0 / acc_ref[h, hd:hd + 1]) for h in range(hpp)],
                            axis=0)
        o_ref[...] = o.T.astype(BF16)


def _attn_prompt(qkv, vt, fq, fk, b, t, fw, hd, tq, tk):
    npair = fw // LANE
    nq, nkb = t // tq, t // tk
    pairs = [(i, j) for i in range(nq) for j in range(((i + 1) * tq - 1) // tk + 1)]
    qt = jnp.asarray(np.array([p[0] for p in pairs], np.int32))
    kt = jnp.asarray(np.array([p[1] for p in pairs], np.int32))
    hpp = LANE // hd
    grid_spec = pltpu.PrefetchScalarGridSpec(
        num_scalar_prefetch=2,
        grid=(b, npair, len(pairs)),
        in_specs=[pl.BlockSpec((tq, LANE), lambda bi, p, s, qt, kt: (bi * nq + qt[s], p)),
                  pl.BlockSpec((tk, LANE), lambda bi, p, s, qt, kt: (bi * nkb + kt[s], npair + p)),
                  pl.BlockSpec((LANE, tk), lambda bi, p, s, qt, kt: (p, bi * nkb + kt[s])),
                  pl.BlockSpec((None, None, hpp, tq), lambda bi, p, s, qt, kt: (bi, p, 0, qt[s])),
                  pl.BlockSpec((None, None, tk, hpp), lambda bi, p, s, qt, kt: (bi, p, kt[s], 0))],
        out_specs=pl.BlockSpec((tq, LANE), lambda bi, p, s, qt, kt: (bi * nq + qt[s], p)),
        scratch_shapes=[pltpu.VMEM((hpp, tq, LANE), BF16),
                        pltpu.VMEM((hpp, 1, tq), F32),
                        pltpu.VMEM((hpp, hd + ONES_ROWS, tq), F32)],
    )
    return pl.pallas_call(
        functools.partial(_attn_kernel, tq=tq, tk=tk, hd=hd),
        grid_spec=grid_spec,
        out_shape=jax.ShapeDtypeStruct((b * t, fw), BF16),
        compiler_params=_cparams(("arbitrary", "arbitrary", "arbitrary")),
        name="fox_attn_prompt",
    )(qt, kt, qkv, qkv, vt, fq, fk)


def _pair_layouts(f2, b, fh, hd):
    t = f2.shape[0]
    hpp = LANE // hd
    f4 = f2.reshape(t, b, fh // hpp, hpp)
    return f4.transpose(1, 2, 3, 0), f4.transpose(1, 2, 0, 3)


def _attn_cache_kernel(q_ref, kn_ref, vn_ref, kt_ref, vt_ref, fq_ref, fkc_ref, fkn_ref, o_ref, *, g, hd):
    s_len = q_ref.shape[0]
    causal = (lax.broadcasted_iota(jnp.int32, (s_len, s_len), 1)
              <= lax.broadcasted_iota(jnp.int32, (s_len, s_len), 0))
    outs = []
    for i in range(g):
        cols = slice(i * hd, (i + 1) * hd)
        q = q_ref[:, cols]
        fq = fq_ref[:, i:i + 1]
        s1 = (jnp.dot(q, kt_ref[i].astype(BF16), preferred_element_type=F32)
              + (fq - fkc_ref[i:i + 1, :]))
        s2 = (lax.dot_general(q, kn_ref[:, cols], NT, preferred_element_type=F32)
              + (fq - fkn_ref[i:i + 1, :]))
        s2 = jnp.where(causal, s2, NEG)
        m = jnp.maximum(jnp.max(s1, axis=-1, keepdims=True), jnp.max(s2, axis=-1, keepdims=True))
        p1 = jnp.exp2(s1 - m)
        p2 = jnp.exp2(s2 - m)
        den = jnp.sum(p1, axis=-1, keepdims=True) + jnp.sum(p2, axis=-1, keepdims=True)
        pv = (lax.dot_general(p1.astype(BF16), vt_ref[i].astype(BF16), NT, preferred_element_type=F32)
              + jnp.dot(p2.astype(BF16), vn_ref[:, cols], preferred_element_type=F32))
        outs.append(pv * (1.0 / den))
    o_ref[...] = jnp.concatenate(outs, axis=1).astype(BF16)


def _attn_cached(qkv, cache_kt, cache_vt, fq, fkc, fkn, l, b, s_len, fh, hd):
    fw = fh * hd
    past = cache_kt.shape[4]
    g = min(CACHE_HEAD_GROUP, fh)
    ng = fh // g
    cspec = pl.BlockSpec((None, None, g, hd, past), lambda bi, gi: (l, bi, gi, 0, 0))
    return pl.pallas_call(
        functools.partial(_attn_cache_kernel, g=g, hd=hd),
        grid=(b, ng),
        in_specs=[pl.BlockSpec((s_len, g * hd), lambda bi, gi: (bi, gi)),
                  pl.BlockSpec((s_len, g * hd), lambda bi, gi: (bi, ng + gi)),
                  pl.BlockSpec((s_len, g * hd), lambda bi, gi: (bi, 2 * ng + gi)),
                  cspec, cspec,
                  pl.BlockSpec((None, None, s_len, g), lambda bi, gi: (bi, gi, 0, 0)),
                  pl.BlockSpec((None, None, g, past), lambda bi, gi: (bi, gi, 0, 0)),
                  pl.BlockSpec((None, None, g, s_len), lambda bi, gi: (bi, gi, 0, 0))],
        out_specs=pl.BlockSpec((s_len, g * hd), lambda bi, gi: (bi, gi)),
        out_shape=jax.ShapeDtypeStruct((b * s_len, fw), BF16),
        compiler_params=_cparams(("arbitrary", "arbitrary")),
        name="fox_attn_cached",
    )(qkv, qkv, qkv, cache_kt, cache_vt, fq, fkc, fkn)


def _hgrn_kernel(q_ref, f_ref, i_ref, g_ref, lb_ref, gn_ref, s0_ref, o_ref, sfin_ref, st_ref,
                 *, c, nc, hh, hk):
    ci = pl.program_id(1)

    @pl.when(ci == 0)
    def _():
        for h in range(hh):
            st_ref[h] = s0_ref[h].astype(F32).T

    lb = lb_ref[...]
    f = lb + (1.0 - lb) * _sigmoid(f_ref[...])
    k_all = 1.0 - f
    g = jnp.log(f)
    cl = g.shape[0]
    hw = hh * hk

    b3 = jnp.dot(_lower_tri(cl), jnp.concatenate(_split3(g), axis=1), preferred_element_type=F32)
    b_all = b3[:, :hw] + b3[:, hw:2 * hw] + b3[:, 2 * hw:]
    q_all, v_all, gate_all = q_ref[...], i_ref[...], g_ref[...]

    sub_s = lax.broadcasted_iota(jnp.int32, (c, c, c), 0)
    sub_t = lax.broadcasted_iota(jnp.int32, (c, c, c), 1)
    sub_n = lax.broadcasted_iota(jnp.int32, (c, c, c), 2)
    pick = (sub_n == sub_s) & (sub_t >= sub_s)
    ys = []
    for h in range(hh):
        sl = slice(h * hk, (h + 1) * hk)
        q, k, v, b = q_all[:, sl], k_all[:, sl], v_all[:, sl], b_all[:, sl]
        vb = v.astype(BF16)
        st = st_ref[h]
        o_inter = lax.dot_general((q * jnp.exp(b)).astype(BF16), st.astype(BF16), NT,
                                  preferred_element_type=F32)
        outs = []
        for i in range(cl // c):
            lo_r, hi_r = i * c, (i + 1) * c
            qi, bi = q[lo_r:hi_r], b[lo_r:hi_r]
            x = jnp.concatenate(
                [qi * jnp.exp(jnp.minimum(bi - b[lo_r + s:lo_r + s + 1], 0.0)) for s in range(c)], axis=0)
            r = lax.dot_general(x.astype(BF16), k[lo_r:hi_r].astype(BF16), NT, preferred_element_type=F32)
            a_diag = jnp.sum(jnp.where(pick, r.reshape(c, c, c), 0.0), axis=0)
            oi = jnp.dot(a_diag.astype(BF16), vb[lo_r:hi_r], preferred_element_type=F32)
            if i > 0:
                ref = b[lo_r - 1:lo_r]
                qd = qi * jnp.exp(bi - ref)
                kdec = k[:lo_r] * jnp.exp(ref - b[:lo_r])
                a_off = lax.dot_general(qd.astype(BF16), kdec.astype(BF16), NT, preferred_element_type=F32)
                oi = oi + jnp.dot(a_off.astype(BF16), vb[:lo_r], preferred_element_type=F32)
            outs.append(oi)
        o = o_inter + jnp.concatenate(outs, axis=0)

        b_last = b[cl - 1:cl]
        k_end = (k * jnp.exp(b_last - b)).astype(BF16)
        st_ref[h] = st * jnp.exp(b_last) + lax.dot_general(vb, k_end, TN, preferred_element_type=F32)

        y = o * lax.rsqrt(jnp.mean(o * o, axis=-1, keepdims=True) + EPS) * gn_ref[...]
        gate = gate_all[:, sl]
        ys.append(y * (gate * _sigmoid(gate)))
    o_ref[...] = jnp.concatenate(ys, axis=1).astype(BF16)

    @pl.when(ci == nc - 1)
    def _():
        for h in range(hh):
            sfin_ref[h] = st_ref[h].T


def _hgrn(proj, lb, gn, s0, b, t, col0, hh, hk):
    cl = _pick(t, 64, HGRN_SUB)
    nc = t // cl
    hw = hh * hk

    def col(kind):
        return pl.BlockSpec((cl, hw), lambda bi, ci: (bi * nc + ci, col0 + kind))

    return pl.pallas_call(
        functools.partial(_hgrn_kernel, c=HGRN_SUB, nc=nc, hh=hh, hk=hk),
        grid=(b, nc),
        in_specs=[col(0), col(1), col(2), col(3),
                  pl.BlockSpec((1, hw), lambda bi, ci: (0, 0)),
                  pl.BlockSpec((1, hk), lambda bi, ci: (0, 0)),
                  pl.BlockSpec((None, hh, hk, hk), lambda bi, ci: (bi, 0, 0, 0))],
        out_specs=[pl.BlockSpec((cl, hw), lambda bi, ci: (bi * nc + ci, 0)),
                   pl.BlockSpec((None, hh, hk, hk), lambda bi, ci: (bi, 0, 0, 0))],
        out_shape=[jax.ShapeDtypeStruct((b * t, hw), BF16),
                   jax.ShapeDtypeStruct((b, hh, hk, hk), F32)],
        scratch_shapes=[pltpu.VMEM((hh, hk, hk), F32)],
        compiler_params=_cparams(("arbitrary", "arbitrary")),
        name="hgrn2",
    )(proj, proj, proj, proj, lb, gn, s0)


def _conv_kernel(ga_ref, gb_ref, buf_ref, w_ref, b_ref, lg_ref, lb_ref, o_ref, nb_ref, xs_ref,
                 *, tb, nblk, width, halo, rb):
    t = pl.program_id(1)
    pad = halo - (width - 1)

    @pl.when(t == 0)
    def _():
        xs_ref[0:halo, :] = jnp.zeros((halo, xs_ref.shape[1]), F32)
        xs_ref[pad:halo, :] = buf_ref[...].astype(F32)

    @pl.when(t > 0)
    def _():
        xs_ref[0:halo, :] = xs_ref[tb:tb + halo, :]

    xs_ref[halo:halo + tb, :] = ga_ref[...] * _sigmoid(gb_ref[...])
    for r0 in range(0, tb, rb):
        acc = xs_ref[r0 + pad:r0 + pad + rb, :] * w_ref[0:1, :]
        for w in range(1, width):
            acc = acc + xs_ref[r0 + pad + w:r0 + pad + w + rb, :] * w_ref[w:w + 1, :]
        y = acc + b_ref[...]
        mu = jnp.mean(y, axis=-1, keepdims=True)
        yc = y - mu
        var = jnp.mean(yc * yc, axis=-1, keepdims=True)
        z = yc * lax.rsqrt(var + EPS) * lg_ref[...] + lb_ref[...]
        o_ref[r0:r0 + rb, :] = (z * _sigmoid(z)).astype(BF16)

    @pl.when(t == nblk - 1)
    def _():
        nb_ref[...] = xs_ref[tb + pad:tb + halo, :]


def _conv(proj, buf, dw_w, dw_b, ln_g, ln_b, b, t, ga_blk, cc):
    width = dw_w.shape[0]
    halo = -(-(width - 1) // 8) * 8
    tb = _pick(t, 256, 8)
    nblk = t // tb
    rb = _pick(tb, 64, 8)
    vec = pl.BlockSpec((1, cc), lambda bi, j: (0, 0))
    return pl.pallas_call(
        functools.partial(_conv_kernel, tb=tb, nblk=nblk, width=width, halo=halo, rb=rb),
        grid=(b, nblk),
        in_specs=[pl.BlockSpec((tb, cc), lambda bi, j: (bi * nblk + j, ga_blk)),
                  pl.BlockSpec((tb, cc), lambda bi, j: (bi * nblk + j, ga_blk + 1)),
                  pl.BlockSpec((None, width - 1, cc), lambda bi, j: (bi, 0, 0)),
                  pl.BlockSpec((width, cc), lambda bi, j: (0, 0)),
                  vec, vec, vec],
        out_specs=[pl.BlockSpec((tb, cc), lambda bi, j: (bi * nblk + j, 0)),
                   pl.BlockSpec((None, width - 1, cc), lambda bi, j: (bi, 0, 0))],
        out_shape=[jax.ShapeDtypeStruct((b * t, cc), BF16),
                   jax.ShapeDtypeStruct((b, width - 1, cc), F32)],
        scratch_shapes=[pltpu.VMEM((halo + tb, cc), F32)],
        compiler_params=_cparams(("arbitrary", "arbitrary")),
        name="conv_module",
    )(proj, proj, buf, dw_w, dw_b, ln_g, ln_b)


def _outproj_kernel(x_ref, gt_ref, fox_ref, hg_ref, cv_ref, w1_ref, w2_ref, w3_ref, o_ref):
    acc = (jnp.dot(fox_ref[...], w1_ref[...].astype(BF16), preferred_element_type=F32)
           + jnp.dot(hg_ref[...], w2_ref[...].astype(BF16), preferred_element_type=F32)
           + jnp.dot(cv_ref[...], w3_ref[...].astype(BF16), preferred_element_type=F32))
    o_ref[...] = _gated_residual(x_ref[...], gt_ref[...], acc, 1.0)


def _outproj(x, mod6, fox, hg, cv, w_out, l, s, r0, seq_len, tm):
    m, d = x.shape
    fw, hw, cc = fox.shape[1], hg.shape[1], cv.shape[1]
    tn = _pick(d, 512, LANE)
    return pl.pallas_call(
        _outproj_kernel,
        grid=(m // tm, d // tn),
        in_specs=[pl.BlockSpec((tm, tn), lambda i, j: (i, j)),
                  _mod_spec(tm, seq_len, tn, (l, s, 2, r0), tiled=True),
                  pl.BlockSpec((tm, fw), lambda i, j: (i, 0)),
                  pl.BlockSpec((tm, hw), lambda i, j: (i, 0)),
                  pl.BlockSpec((tm, cc), lambda i, j: (i, 0)),
                  pl.BlockSpec((None, fw, tn), lambda i, j: (l, 0, j)),
                  pl.BlockSpec((None, hw, tn), lambda i, j: (l, fw // hw, j)),
                  pl.BlockSpec((None, cc, tn), lambda i, j: (l, (fw + hw) // cc, j))],
        out_specs=pl.BlockSpec((tm, tn), lambda i, j: (i, j)),
        out_shape=jax.ShapeDtypeStruct((m, d), F32),
        compiler_params=_cparams(("arbitrary", "arbitrary")),
        name="mix_outproj",
    )(x, mod6, fox, hg, cv, w_out, w_out, w_out)


def _norm_kernel(x_ref, g_ref, o_ref):
    x = x_ref[...]
    o_ref[...] = x * lax.rsqrt(jnp.mean(x * x, axis=-1, keepdims=True) + EPS) * g_ref[...]


def _final_norm(x, g, tm):
    m, d = x.shape
    return pl.pallas_call(
        _norm_kernel,
        grid=(m // tm,),
        in_specs=[pl.BlockSpec((tm, d), lambda i: (i, 0)), pl.BlockSpec((1, d), lambda i: (0, 0))],
        out_specs=pl.BlockSpec((tm, d), lambda i: (i, 0)),
        out_shape=jax.ShapeDtypeStruct((m, d), F32),
        compiler_params=_cparams(("arbitrary",)),
        name="final_norm",
    )(x, g)


def _time_major(a, b, t):
    h = a.shape[-1]
    return a.reshape(b, t, h).transpose(1, 0, 2).reshape(t, b * h)


def _batch_major(a, b):
    t = a.shape[0]
    return a.reshape(t, b, -1).transpose(1, 0, 2)


def kernel(x_prompt, x_sample, cache_fox_k, cache_fox_v, cache_fox_logf, state_hgrn, state_conv,
           c_prompt, c_sample, norm_g, ada_w, ada_b, ffn_w_in, ffn_w_out, mix_w_in, mix_w_out,
           fox_f_bias, hgrn_lb_logits, hgrn_norm_g, conv_dw_w, conv_dw_b, conv_ln_g, conv_ln_b,
           final_norm_g):
    bp, seq, d = x_prompt.shape
    bs, dseq, _ = x_sample.shape
    depth = norm_g.shape[0]
    nsub = norm_g.shape[1]
    past = cache_fox_k.shape[2]
    fh, hd = cache_fox_k.shape[3], cache_fox_k.shape[4]
    fw = fh * hd
    hh, hk = state_hgrn.shape[2], state_hgrn.shape[3]
    hw = hh * hk
    cc = conv_dw_b.shape[1]
    width = conv_dw_w.shape[1]
    qscale = float(hd) ** -0.5 * LOG2E

    lb_sm = jnp.cumsum(jax.nn.softmax(hgrn_lb_logits.astype(F32), axis=0), axis=0)
    lb_all = lb_sm - lb_sm[0:1]

    row0_sample = -(-bp // 16) * 16
    c_rows = jnp.concatenate([jnp.pad(c_prompt, ((0, row0_sample - bp), (0, 0))),
                              jnp.pad(c_sample, ((0, -bs % 16), (0, 0)))], axis=0)
    mod6 = _ada(jax.nn.silu(c_rows).astype(BF16), ada_w, ada_b).reshape(depth, -1, nsub, 3, 1, d)
    g3 = norm_g.reshape(depth * nsub, 1, d)

    tn_in = _pick(fw, 512, LANE)
    n_rest = mix_w_in.shape[2] - 3 * fw - fh
    assert n_rest % tn_in == 0 and fh <= tn_in
    w_rest = jnp.concatenate(
        [mix_w_in[:, :, 3 * fw + fh:], mix_w_in[:, :, 3 * fw:3 * fw + fh],
         jnp.zeros((depth, d, tn_in - fh), F32)], axis=2)
    fz0 = 3 * fw + n_rest
    blk_hgrn = 3 * fw // hw
    blk_conv = (3 * fw + 4 * hw) // cc
    ffn_bf16 = {}
    cache_kt = cache_fox_k.transpose(0, 1, 3, 4, 2)
    cache_vt = cache_fox_v.transpose(0, 1, 3, 4, 2)
    cgrp = min(CACHE_HEAD_GROUP, fh)

    def trunk(x, b, t, row0, cached):
        m = b * t
        tm = _pick(m, 1024, 8) if t >= 1024 else _pick(m, 512, t)
        tm_ffn = _pick(m, FFN_TM, 8) if t >= FFN_TM else tm
        tq = _pick(t, 1024, LANE)
        bias_row = lambda l: jnp.tile(fox_f_bias[l], b)[None]

        def ffn(x, l, s):
            if cached:
                x, *ffn_bf16[l, s] = _ffn(x, g3, mod6, ffn_w_in, ffn_w_in, ffn_w_out, l, s, row0, t,
                                          tm_ffn, True)
                return x
            return _ffn(x, g3, mod6, *ffn_bf16[l, s], l, s, row0, t, tm_ffn, False)[0]

        new = []
        for l in range(depth):
            x = ffn(x, l, 0)
            proj, qkv = _inproj(x, g3, mod6, mix_w_in, w_rest, l, 1, row0, fw, tn_in, qscale, t, tm)
            k_new = proj[:, fw:2 * fw].reshape(b, t, fh, hd)
            v_new = proj[:, 2 * fw:3 * fw].reshape(b, t, fh, hd)
            logf_tm, fcum = _logf_cumsum(_time_major(proj[:, fz0:fz0 + fh], b, t), bias_row(l), True)
            logf = _batch_major(logf_tm, b)
            if not cached:
                fq, fk = _pair_layouts(fcum * LOG2E, b, fh, hd)
                fox = _attn_prompt(qkv, qkv[:, 2 * fw:].T, fq, fk, b, t, fw, hd, tq, tq)
                s0 = jnp.zeros((b, hh, hk, hk), F32)
                buf = jnp.zeros((b, width - 1, cc), F32)
            else:
                cl_tm = cache_fox_logf[l].astype(F32).transpose(1, 0, 2).reshape(past, b * fh)
                _, fc = _logf_cumsum(cl_tm, jnp.zeros((1, b * fh), F32), False)
                f2_new = _batch_major((fcum + fc[-1:]) * LOG2E, b).reshape(b, t, fh // cgrp, cgrp)
                f2_old = _batch_major(fc * LOG2E, b).reshape(b, past, fh // cgrp, cgrp)
                fox = _attn_cached(qkv, cache_kt, cache_vt, f2_new.transpose(0, 2, 1, 3),
                                   f2_old.transpose(0, 2, 3, 1), f2_new.transpose(0, 2, 3, 1), l, b, t, fh, hd)
                s0 = state_hgrn[l]
                buf = state_conv[l]
            hg, s_fin = _hgrn(proj, lb_all[l][None], hgrn_norm_g[l][None], s0, b, t, blk_hgrn, hh, hk)
            cv, new_buf = _conv(proj, buf, conv_dw_w[l], conv_dw_b[l][None], conv_ln_g[l][None],
                                conv_ln_b[l][None], b, t, blk_conv, cc)
            x = _outproj(x, mod6, fox, hg, cv, mix_w_out, l, 1, row0, t, tm)
            x = ffn(x, l, 2)
            new.append((k_new, v_new, logf, s_fin, new_buf))
        y = _final_norm(x, final_norm_g[None], tm).reshape(b, t, d)
        return y, [jnp.stack([s[i] for s in new]) for i in range(5)]

    y_s, (ks, vs, fs, hs, cs) = trunk(x_sample.reshape(bs * dseq, d), bs, dseq, row0_sample, True)
    y_p, (kp, vp, fp, hp, cp) = trunk(x_prompt.reshape(bp * seq, d), bp, seq, 0, False)
    return (y_p, y_s, kp, vp, fp, hp, cp, ks, vs, fs, hs, cs)
```

```python
import functools

import jax
import jax.numpy as jnp
import numpy as np
from jax import lax
from jax.experimental import pallas as pl
from jax.experimental.pallas import tpu as pltpu

F32 = jnp.float32
BF16 = jnp.bfloat16
EPS = 1e-6
NEG = -1e30
LOG2E = 1.4426950408889634
LANE = 128
ONES_ROWS = 16
ATTN_QCHUNK = 512
HGRN_CHUNK = 64
CACHE_HEAD_GROUP = 4
HGRN_SUB = 16
FFN_TM, FFN_TF = 512, 512
FFN_TF_EMIT = 256
VMEM_LIMIT = 60 * 1024 * 1024

NT = (((1,), (1,)), ((), ()))
TN = (((0,), (0,)), ((), ()))


def _pick(n, target, mult=8):
    for t in range(min(n, target), 0, -1):
        if n % t == 0 and t % mult == 0:
            return t
    return n


def _cparams(sem):
    return pltpu.CompilerParams(dimension_semantics=sem, vmem_limit_bytes=VMEM_LIMIT)


def _sigmoid(x):
    return 1.0 / (1.0 + jnp.exp(-x))


def _split3(x):
    hi = x.astype(BF16)
    r1 = x - hi.astype(F32)
    mid = r1.astype(BF16)
    lo = (r1 - mid.astype(F32)).astype(BF16)
    return hi, mid, lo


def _lower_tri(n):
    return (lax.broadcasted_iota(jnp.int32, (n, n), 0)
            >= lax.broadcasted_iota(jnp.int32, (n, n), 1)).astype(BF16)


def _modulate(x, g, shift, scale):
    tm, d = x.shape
    grp = shift.shape[0]
    y = x * lax.rsqrt(jnp.mean(x * x, axis=-1, keepdims=True) + EPS) * g
    y = y.reshape(grp, tm // grp, d) * (1.0 + scale) + shift
    return y.reshape(tm, d)


def _gated_residual(x, gate, upd, coef):
    tm, d = x.shape
    grp = gate.shape[0]
    y = x.reshape(grp, tm // grp, d) + (coef * gate) * upd.reshape(grp, tm // grp, d)
    return y.reshape(tm, d)


def _mod_spec(tm, seq_len, tn, where, tiled=False):
    l, s, kind, r0 = where
    grp = max(1, tm // seq_len)
    assert r0 % grp == 0
    return pl.BlockSpec(
        (None, grp, None, None, 1, tn),
        lambda i, j: (l, r0 // grp + (i * tm) // (seq_len * grp), s, kind, 0, j if tiled else 0))


def _vec_spec(n, idx):
    return pl.BlockSpec((None, 1, n), lambda *_: (idx, 0, 0))


def _ada_kernel(c_ref, w_ref, b_ref, o_ref):
    w = w_ref[...].astype(BF16)
    o_ref[...] = jnp.dot(c_ref[...], w, preferred_element_type=F32) + b_ref[...]


def _ada(c_act, ada_w, ada_b):
    depth, d, n = ada_w.shape
    r = c_act.shape[0]
    tn = _pick(n, 1024, LANE)
    return pl.pallas_call(
        _ada_kernel,
        grid=(depth, n // tn),
        in_specs=[pl.BlockSpec((r, d), lambda l, j: (0, 0)),
                  pl.BlockSpec((None, d, tn), lambda l, j: (l, 0, j)),
                  pl.BlockSpec((None, 1, tn), lambda l, j: (l, 0, j))],
        out_specs=pl.BlockSpec((None, r, tn), lambda l, j: (l, 0, j)),
        out_shape=jax.ShapeDtypeStruct((depth, r, n), F32),
        compiler_params=_cparams(("arbitrary", "arbitrary")),
        name="ada_proj",
    )(c_act, ada_w, ada_b.reshape(depth, 1, n))


def _ffn_kernel(x_ref, g_ref, sh_ref, sc_ref, gt_ref, wa_ref, wb_ref, wo_ref, o_ref, *rest, nj, emit):
    if emit:
        wab_ref, wbb_ref, wob_ref, h_ref = rest
    else:
        (h_ref,) = rest
    j = pl.program_id(1)

    @pl.when(j == 0)
    def _():
        h_ref[...] = _modulate(x_ref[...], g_ref[...], sh_ref[...], sc_ref[...]).astype(BF16)
        o_ref[...] = jnp.zeros_like(o_ref)

    wa, wb, wo = wa_ref[...], wb_ref[...], wo_ref[...]
    if emit:
        wa, wb, wo = wa.astype(BF16), wb.astype(BF16), wo.astype(BF16)
        wab_ref[...] = wa
        wbb_ref[...] = wb
        wob_ref[...] = wo
    h = h_ref[...]
    a = jnp.dot(h, wa, preferred_element_type=F32)
    b = jnp.dot(h, wb, preferred_element_type=F32)
    act = (a * _sigmoid(a) * b).astype(BF16)
    o_ref[...] += jnp.dot(act, wo, preferred_element_type=F32)

    @pl.when(j == nj - 1)
    def _():
        o_ref[...] = _gated_residual(x_ref[...], gt_ref[...], o_ref[...], 0.5)


def _ffn(x, g3, mod6, wa, wb, wo, l, s, r0, seq_len, tm, emit):
    m, d = x.shape
    dff = wo.shape[-2]
    tf = _pick(dff, FFN_TF_EMIT if emit else FFN_TF, LANE)
    nj = dff // tf
    nsub = mod6.shape[2]
    mods = [_mod_spec(tm, seq_len, d, (l, s, kind, r0)) for kind in range(3)]
    if emit:
        assert m == tm
        w_specs = [pl.BlockSpec((None, None, d, tf), lambda i, j: (l, s // 2, 0, j)),
                   pl.BlockSpec((None, None, d, tf), lambda i, j: (l, s // 2, 0, j + nj)),
                   pl.BlockSpec((None, None, tf, d), lambda i, j: (l, s // 2, j, 0))]
    bf_specs = [pl.BlockSpec((d, tf), lambda i, j: (0, j)),
                pl.BlockSpec((d, tf), lambda i, j: (0, j)),
                pl.BlockSpec((tf, d), lambda i, j: (j, 0))]
    if not emit:
        w_specs = bf_specs
    out_specs = [pl.BlockSpec((tm, d), lambda i, j: (i, 0))]
    out_shape = [jax.ShapeDtypeStruct((m, d), F32)]
    if emit:
        out_specs += bf_specs
        out_shape += [jax.ShapeDtypeStruct((d, dff), BF16)] * 2 + [jax.ShapeDtypeStruct((dff, d), BF16)]
    return pl.pallas_call(
        functools.partial(_ffn_kernel, nj=nj, emit=emit),
        grid=(m // tm, nj),
        in_specs=[pl.BlockSpec((tm, d), lambda i, j: (i, 0)), _vec_spec(d, l * nsub + s)] + mods + w_specs,
        out_specs=out_specs,
        out_shape=out_shape,
        scratch_shapes=[pltpu.VMEM((tm, d), BF16)],
        compiler_params=_cparams(("arbitrary", "arbitrary")),
        name="ffn",
    )(x, g3, mod6, mod6, mod6, wa, wb, wo)


def _inproj_kernel(x_ref, g_ref, sh_ref, sc_ref, wq_ref, wr_ref, proj_ref, qkv_ref, h_ref,
                   *, nq, nqkv, qscale):
    j = pl.program_id(1)

    @pl.when(j == 0)
    def _():
        h_ref[...] = _modulate(x_ref[...], g_ref[...], sh_ref[...], sc_ref[...]).astype(BF16)

    @pl.when(j < nqkv)
    def _():
        r = jnp.dot(h_ref[...], wq_ref[...].astype(BF16), preferred_element_type=F32)
        proj_ref[...] = r
        qkv_ref[...] = (r * jnp.where(j < nq, qscale, 1.0)).astype(BF16)

    @pl.when(j >= nqkv)
    def _():
        proj_ref[...] = jnp.dot(h_ref[...], wr_ref[...].astype(BF16), preferred_element_type=F32)


def _inproj(x, g3, mod6, w_in, w_rest, l, s, r0, fw, tn, qscale, seq_len, tm):
    m, d = x.shape
    nq, nqkv = fw // tn, 3 * fw // tn
    n_pad = 3 * fw + w_rest.shape[2]
    nsub = mod6.shape[2]
    return pl.pallas_call(
        functools.partial(_inproj_kernel, nq=nq, nqkv=nqkv, qscale=qscale),
        grid=(m // tm, n_pad // tn),
        in_specs=[pl.BlockSpec((tm, d), lambda i, j: (i, 0)), _vec_spec(d, l * nsub + s),
                  _mod_spec(tm, seq_len, d, (l, s, 0, r0)), _mod_spec(tm, seq_len, d, (l, s, 1, r0)),
                  pl.BlockSpec((None, d, tn), lambda i, j: (l, 0, jnp.minimum(j, nqkv - 1))),
                  pl.BlockSpec((None, d, tn), lambda i, j: (l, 0, jnp.maximum(j - nqkv, 0)))],
        out_specs=[pl.BlockSpec((tm, tn), lambda i, j: (i, j)),
                   pl.BlockSpec((tm, tn), lambda i, j: (i, jnp.minimum(j, nqkv - 1)))],
        out_shape=[jax.ShapeDtypeStruct((m, n_pad), F32),
                   jax.ShapeDtypeStruct((m, 3 * fw), BF16)],
        scratch_shapes=[pltpu.VMEM((tm, d), BF16)],
        compiler_params=_cparams(("arbitrary", "arbitrary")),
        name="mix_inproj",
    )(x, g3, mod6, mod6, w_in, w_rest)


def _cumsum_kernel(z_ref, bias_ref, lf_ref, f_ref, carry_ref, *, logsig):
    t = pl.program_id(0)

    @pl.when(t == 0)
    def _():
        carry_ref[...] = jnp.zeros_like(carry_ref)

    z = z_ref[...]
    if logsig:
        z = z + bias_ref[...]
        z = jnp.minimum(z, 0.0) - jnp.log(1.0 + jnp.exp(-jnp.abs(z)))
    lf_ref[...] = z
    tb = z.shape[0]
    tri = _lower_tri(tb)
    hi, mid, lo = _split3(z)
    cs = (jnp.dot(tri, hi, preferred_element_type=F32) + jnp.dot(tri, mid, preferred_element_type=F32)
          + jnp.dot(tri, lo, preferred_element_type=F32))
    f = cs + carry_ref[...]
    f_ref[...] = f
    carry_ref[...] = f[tb - 1:tb, :]


def _logf_cumsum(z, bias, logsig):
    t, n = z.shape
    tb = _pick(t, 512, 8)
    spec = pl.BlockSpec((tb, n), lambda i: (i, 0))
    return pl.pallas_call(
        functools.partial(_cumsum_kernel, logsig=logsig),
        grid=(t // tb,),
        in_specs=[spec, pl.BlockSpec((1, n), lambda i: (0, 0))],
        out_specs=[spec, spec],
        out_shape=[jax.ShapeDtypeStruct((t, n), F32)] * 2,
        scratch_shapes=[pltpu.VMEM((1, n), F32)],
        compiler_params=_cparams(("arbitrary",)),
        name="logf_cumsum",
    )(z, bias)


def _attn_kernel(qt_ref, kt_ref, q_ref, k_ref, vt_ref, fq_ref, fk_ref, o_ref,
                 qm_ref, m_ref, acc_ref, *, tq, tk, hd):
    t = pl.program_id(2)
    qi, ki = qt_ref[t], kt_ref[t]
    hpp = LANE // hd
    ones = jnp.ones((ONES_ROWS, tk), BF16)

    @pl.when(ki == 0)
    def _():
        lane = lax.broadcasted_iota(jnp.int32, (1, LANE), 1)
        q = q_ref[...]
        for h in range(hpp):
            qm_ref[h] = jnp.where((lane >= hd * h) & (lane < hd * (h + 1)), q, 0)
        m_ref[...] = jnp.full_like(m_ref, NEG)
        acc_ref[...] = jnp.zeros_like(acc_ref)

    def body(masked):
        kk = k_ref[...]
        cw = min(ATTN_QCHUNK, tq)
        if masked:
            rel = (lax.broadcasted_iota(jnp.int32, (tk, cw), 0)
                   - lax.broadcasted_iota(jnp.int32, (tk, cw), 1))
        chains = [(h, slice(c0, c0 + cw), c0) for c0 in range(0, tq, cw) for h in range(hpp)]
        state = [(m_ref[h, :, cols], acc_ref[h, :, cols]) for h, cols, _ in chains]
        new_state = []
        scores = [lax.dot_general(kk, qm_ref[h, cols, :], NT, preferred_element_type=F32)
                  for h, cols, _ in chains]
        for (h, cols, c0), (m_prev, acc_prev), s in zip(chains, state, scores):
            u = s - fk_ref[:, h:h + 1]
            if masked:
                u = jnp.where(rel <= qi * tq + c0 - ki * tk, u, NEG)
            fq = fq_ref[h:h + 1, cols]
            m_new = jnp.maximum(m_prev, jnp.max(u, axis=0, keepdims=True) + fq)
            alpha = jnp.exp2(m_prev - m_new)
            p = jnp.exp2(u + (fq - m_new)).astype(BF16)
            v_one = jnp.concatenate([vt_ref[h * hd:(h + 1) * hd, :], ones], axis=0)
            new_state.append((m_new, alpha * acc_prev + jnp.dot(v_one, p, preferred_element_type=F32)))
        for (h, cols, _), (m_new, acc_new) in zip(chains, new_state):
            m_ref[h, :, cols] = m_new
            acc_ref[h, :, cols] = acc_new

    needs_mask = (ki + 1) * tk - 1 > qi * tq

    @pl.when(needs_mask)
    def _():
        body(True)

    @pl.when(jnp.logical_not(needs_mask))
    def _():
        body(False)

    @pl.when((ki + 1) * tk >= (qi + 1) * tq)
    def _():
        o = jnp.concatenate([acc_ref[h, 0:hd] * (1.0 / acc_ref[h, hd:hd + 1]) for h in range(hpp)],
                            axis=0)
        o_ref[...] = o.T.astype(BF16)


def _attn_prompt(qkv, vt, fq, fk, b, t, fw, hd, tq, tk):
    npair = fw // LANE
    nq, nkb = t // tq, t // tk
    pairs = [(i, j) for i in range(nq) for j in range(((i + 1) * tq - 1) // tk + 1)]
    qt = jnp.asarray(np.array([p[0] for p in pairs], np.int32))
    kt = jnp.asarray(np.array([p[1] for p in pairs], np.int32))
    hpp = LANE // hd
    grid_spec = pltpu.PrefetchScalarGridSpec(
        num_scalar_prefetch=2,
        grid=(b, npair, len(pairs)),
        in_specs=[pl.BlockSpec((tq, LANE), lambda bi, p, s, qt, kt: (bi * nq + qt[s], p)),
                  pl.BlockSpec((tk, LANE), lambda bi, p, s, qt, kt: (bi * nkb + kt[s], npair + p)),
                  pl.BlockSpec((LANE, tk), lambda bi, p, s, qt, kt: (p, bi * nkb + kt[s])),
                  pl.BlockSpec((None, None, hpp, tq), lambda bi, p, s, qt, kt: (bi, p, 0, qt[s])),
                  pl.BlockSpec((None, None, tk, hpp), lambda bi, p, s, qt, kt: (bi, p, kt[s], 0))],
        out_specs=pl.BlockSpec((tq, LANE), lambda bi, p, s, qt, kt: (bi * nq + qt[s], p)),
        scratch_shapes=[pltpu.VMEM((hpp, tq, LANE), BF16),
                        pltpu.VMEM((hpp, 1, tq), F32),
                        pltpu.VMEM((hpp, hd + ONES_ROWS, tq), F32)],
    )
    return pl.pallas_call(
        functools.partial(_attn_kernel, tq=tq, tk=tk, hd=hd),
        grid_spec=grid_spec,
        out_shape=jax.ShapeDtypeStruct((b * t, fw), BF16),
        compiler_params=_cparams(("arbitrary", "arbitrary", "arbitrary")),
        name="fox_attn_prompt",
    )(qt, kt, qkv, qkv, vt, fq, fk)


def _pair_layouts(f2, b, fh, hd):
    t = f2.shape[0]
    hpp = LANE // hd
    f4 = f2.reshape(t, b, fh // hpp, hpp)
    return f4.transpose(1, 2, 3, 0), f4.transpose(1, 2, 0, 3)


def _attn_cache_kernel(q_ref, kn_ref, vn_ref, kt_ref, vt_ref, fq_ref, fkc_ref, fkn_ref, o_ref, *, g, hd):
    s_len = q_ref.shape[0]
    causal = (lax.broadcasted_iota(jnp.int32, (s_len, s_len), 1)
              <= lax.broadcasted_iota(jnp.int32, (s_len, s_len), 0))
    heads = range(g)
    cols = [slice(i * hd, (i + 1) * hd) for i in heads]
    q = [q_ref[:, c] for c in cols]
    s1 = [jnp.dot(q[i], kt_ref[i].astype(BF16), preferred_element_type=F32) for i in heads]
    s2 = [lax.dot_general(q[i], kn_ref[:, cols[i]], NT, preferred_element_type=F32) for i in heads]
    p1, p2, den = [], [], []
    for i in heads:
        fq = fq_ref[:, i:i + 1]
        a1 = s1[i] + (fq - fkc_ref[i:i + 1, :])
        a2 = jnp.where(causal, s2[i] + (fq - fkn_ref[i:i + 1, :]), NEG)
        m = jnp.maximum(jnp.max(a1, axis=-1, keepdims=True), jnp.max(a2, axis=-1, keepdims=True))
        e1 = jnp.exp2(a1 - m)
        e2 = jnp.exp2(a2 - m)
        den.append(jnp.sum(e1, axis=-1, keepdims=True) + jnp.sum(e2, axis=-1, keepdims=True))
        p1.append(e1.astype(BF16))
        p2.append(e2.astype(BF16))
    pv = [lax.dot_general(p1[i], vt_ref[i].astype(BF16), NT, preferred_element_type=F32)
          + jnp.dot(p2[i], vn_ref[:, cols[i]], preferred_element_type=F32) for i in heads]
    o_ref[...] = jnp.concatenate([pv[i] * (1.0 / den[i]) for i in heads], axis=1).astype(BF16)


def _attn_cached(qkv, cache_kt, cache_vt, fq, fkc, fkn, l, b, s_len, fh, hd):
    fw = fh * hd
    past = cache_kt.shape[4]
    g = min(CACHE_HEAD_GROUP, fh)
    ng = fh // g
    cspec = pl.BlockSpec((None, None, g, hd, past), lambda bi, gi: (l, bi, gi, 0, 0))
    return pl.pallas_call(
        functools.partial(_attn_cache_kernel, g=g, hd=hd),
        grid=(b, ng),
        in_specs=[pl.BlockSpec((s_len, g * hd), lambda bi, gi: (bi, gi)),
                  pl.BlockSpec((s_len, g * hd), lambda bi, gi: (bi, ng + gi)),
                  pl.BlockSpec((s_len, g * hd), lambda bi, gi: (bi, 2 * ng + gi)),
                  cspec, cspec,
                  pl.BlockSpec((None, None, s_len, g), lambda bi, gi: (bi, gi, 0, 0)),
                  pl.BlockSpec((None, None, g, past), lambda bi, gi: (bi, gi, 0, 0)),
                  pl.BlockSpec((None, None, g, s_len), lambda bi, gi: (bi, gi, 0, 0))],
        out_specs=pl.BlockSpec((s_len, g * hd), lambda bi, gi: (bi, gi)),
        out_shape=jax.ShapeDtypeStruct((b * s_len, fw), BF16),
        compiler_params=_cparams(("arbitrary", "arbitrary")),
        name="fox_attn_cached",
    )(qkv, qkv, qkv, cache_kt, cache_vt, fq, fkc, fkn)


def _hgrn_kernel(q_ref, f_ref, i_ref, g_ref, lb_ref, gn_ref, s0_ref, o_ref, sfin_ref, st_ref,
                 *, c, nc, hh, hk):
    ci = pl.program_id(1)

    @pl.when(ci == 0)
    def _():
        for h in range(hh):
            st_ref[h] = s0_ref[h].astype(F32).T

    lb = lb_ref[...]
    f = lb + (1.0 - lb) * _sigmoid(f_ref[...])
    k_all = 1.0 - f
    g = jnp.log(f)
    cl = g.shape[0]
    hw = hh * hk

    b3 = jnp.dot(_lower_tri(cl), jnp.concatenate(_split3(g), axis=1), preferred_element_type=F32)
    b_all = b3[:, :hw] + b3[:, hw:2 * hw] + b3[:, 2 * hw:]
    q_all, v_all, gate_all = q_ref[...], i_ref[...], g_ref[...]

    sub_s = lax.broadcasted_iota(jnp.int32, (c, c, c), 0)
    sub_t = lax.broadcasted_iota(jnp.int32, (c, c, c), 1)
    sub_n = lax.broadcasted_iota(jnp.int32, (c, c, c), 2)
    pick = (sub_n == sub_s) & (sub_t >= sub_s)
    heads = range(hh)
    sls = [slice(h * hk, (h + 1) * hk) for h in heads]
    q = [q_all[:, sl] for sl in sls]
    k = [k_all[:, sl] for sl in sls]
    b = [b_all[:, sl] for sl in sls]
    kb = [x.astype(BF16) for x in k]
    vb = [v_all[:, sl].astype(BF16) for sl in sls]
    st = [st_ref[h] for h in heads]
    b_last = [x[cl - 1:cl] for x in b]

    k_end = [(k[h] * jnp.exp(b_last[h] - b[h])).astype(BF16) for h in heads]
    upd = [lax.dot_general(vb[h], k_end[h], TN, preferred_element_type=F32) for h in heads]
    o_inter = [lax.dot_general((q[h] * jnp.exp(b[h])).astype(BF16), st[h].astype(BF16), NT,
                               preferred_element_type=F32) for h in heads]
    for h in heads:
        st_ref[h] = st[h] * jnp.exp(b_last[h]) + upd[h]

    outs = [[] for _ in heads]
    for i in range(cl // c):
        lo_r, hi_r = i * c, (i + 1) * c
        qi = [x[lo_r:hi_r] for x in q]
        bi = [x[lo_r:hi_r] for x in b]
        x = [jnp.concatenate(
            [qi[h] * jnp.exp(jnp.minimum(bi[h] - b[h][lo_r + s:lo_r + s + 1], 0.0)) for s in range(c)],
            axis=0).astype(BF16) for h in heads]
        r = [lax.dot_general(x[h], kb[h][lo_r:hi_r], NT, preferred_element_type=F32) for h in heads]
        a_diag = [jnp.sum(jnp.where(pick, r[h].reshape(c, c, c), 0.0), axis=0).astype(BF16)
                  for h in heads]
        oi = [jnp.dot(a_diag[h], vb[h][lo_r:hi_r], preferred_element_type=F32) for h in heads]
        if i > 0:
            ref = [x_[lo_r - 1:lo_r] for x_ in b]
            qd = [(qi[h] * jnp.exp(bi[h] - ref[h])).astype(BF16) for h in heads]
            kdec = [(k[h][:lo_r] * jnp.exp(ref[h] - b[h][:lo_r])).astype(BF16) for h in heads]
            a_off = [lax.dot_general(qd[h], kdec[h], NT, preferred_element_type=F32).astype(BF16)
                     for h in heads]
            oi = [oi[h] + jnp.dot(a_off[h], vb[h][:lo_r], preferred_element_type=F32) for h in heads]
        for h in heads:
            outs[h].append(oi[h])

    ys = []
    for h in heads:
        o = o_inter[h] + jnp.concatenate(outs[h], axis=0)
        y = o * lax.rsqrt(jnp.mean(o * o, axis=-1, keepdims=True) + EPS) * gn_ref[...]
        gate = gate_all[:, sls[h]]
        ys.append(y * (gate * _sigmoid(gate)))
    o_ref[...] = jnp.concatenate(ys, axis=1).astype(BF16)

    @pl.when(ci == nc - 1)
    def _():
        for h in range(hh):
            sfin_ref[h] = st_ref[h].T


def _hgrn(proj, lb, gn, s0, b, t, col0, hh, hk):
    cl = _pick(t, HGRN_CHUNK, HGRN_SUB)
    nc = t // cl
    hw = hh * hk

    def col(kind):
        return pl.BlockSpec((cl, hw), lambda bi, ci: (bi * nc + ci, col0 + kind))

    return pl.pallas_call(
        functools.partial(_hgrn_kernel, c=HGRN_SUB, nc=nc, hh=hh, hk=hk),
        grid=(b, nc),
        in_specs=[col(0), col(1), col(2), col(3),
                  pl.BlockSpec((1, hw), lambda bi, ci: (0, 0)),
                  pl.BlockSpec((1, hk), lambda bi, ci: (0, 0)),
                  pl.BlockSpec((None, hh, hk, hk), lambda bi, ci: (bi, 0, 0, 0))],
        out_specs=[pl.BlockSpec((cl, hw), lambda bi, ci: (bi * nc + ci, 0)),
                   pl.BlockSpec((None, hh, hk, hk), lambda bi, ci: (bi, 0, 0, 0))],
        out_shape=[jax.ShapeDtypeStruct((b * t, hw), BF16),
                   jax.ShapeDtypeStruct((b, hh, hk, hk), F32)],
        scratch_shapes=[pltpu.VMEM((hh, hk, hk), F32)],
        compiler_params=_cparams(("arbitrary", "arbitrary")),
        name="hgrn2",
    )(proj, proj, proj, proj, lb, gn, s0)


def _conv_kernel(ga_ref, gb_ref, buf_ref, w_ref, b_ref, lg_ref, lb_ref, o_ref, nb_ref, xs_ref,
                 *, tb, nblk, width, halo, rb):
    t = pl.program_id(1)
    pad = halo - (width - 1)

    @pl.when(t == 0)
    def _():
        xs_ref[0:halo, :] = jnp.zeros((halo, xs_ref.shape[1]), F32)
        xs_ref[pad:halo, :] = buf_ref[...].astype(F32)

    @pl.when(t > 0)
    def _():
        xs_ref[0:halo, :] = xs_ref[tb:tb + halo, :]

    xs_ref[halo:halo + tb, :] = ga_ref[...] * _sigmoid(gb_ref[...])
    for r0 in range(0, tb, rb):
        acc = xs_ref[r0 + pad:r0 + pad + rb, :] * w_ref[0:1, :]
        for w in range(1, width):
            acc = acc + xs_ref[r0 + pad + w:r0 + pad + w + rb, :] * w_ref[w:w + 1, :]
        y = acc + b_ref[...]
        mu = jnp.mean(y, axis=-1, keepdims=True)
        yc = y - mu
        var = jnp.mean(yc * yc, axis=-1, keepdims=True)
        z = yc * lax.rsqrt(var + EPS) * lg_ref[...] + lb_ref[...]
        o_ref[r0:r0 + rb, :] = (z * _sigmoid(z)).astype(BF16)

    @pl.when(t == nblk - 1)
    def _():
        nb_ref[...] = xs_ref[tb + pad:tb + halo, :]


def _conv(proj, buf, dw_w, dw_b, ln_g, ln_b, b, t, ga_blk, cc):
    width = dw_w.shape[0]
    halo = -(-(width - 1) // 8) * 8
    tb = _pick(t, 256, 8)
    nblk = t // tb
    rb = _pick(tb, 64, 8)
    vec = pl.BlockSpec((1, cc), lambda bi, j: (0, 0))
    return pl.pallas_call(
        functools.partial(_conv_kernel, tb=tb, nblk=nblk, width=width, halo=halo, rb=rb),
        grid=(b, nblk),
        in_specs=[pl.BlockSpec((tb, cc), lambda bi, j: (bi * nblk + j, ga_blk)),
                  pl.BlockSpec((tb, cc), lambda bi, j: (bi * nblk + j, ga_blk + 1)),
                  pl.BlockSpec((None, width - 1, cc), lambda bi, j: (bi, 0, 0)),
                  pl.BlockSpec((width, cc), lambda bi, j: (0, 0)),
                  vec, vec, vec],
        out_specs=[pl.BlockSpec((tb, cc), lambda bi, j: (bi * nblk + j, 0)),
                   pl.BlockSpec((None, width - 1, cc), lambda bi, j: (bi, 0, 0))],
        out_shape=[jax.ShapeDtypeStruct((b * t, cc), BF16),
                   jax.ShapeDtypeStruct((b, width - 1, cc), F32)],
        scratch_shapes=[pltpu.VMEM((halo + tb, cc), F32)],
        compiler_params=_cparams(("arbitrary", "arbitrary")),
        name="conv_module",
    )(proj, proj, buf, dw_w, dw_b, ln_g, ln_b)


def _outproj_kernel(x_ref, gt_ref, fox_ref, hg_ref, cv_ref, w1_ref, w2_ref, w3_ref, o_ref):
    acc = (jnp.dot(fox_ref[...], w1_ref[...].astype(BF16), preferred_element_type=F32)
           + jnp.dot(hg_ref[...], w2_ref[...].astype(BF16), preferred_element_type=F32)
           + jnp.dot(cv_ref[...], w3_ref[...].astype(BF16), preferred_element_type=F32))
    o_ref[...] = _gated_residual(x_ref[...], gt_ref[...], acc, 1.0)


def _outproj(x, mod6, fox, hg, cv, w_out, l, s, r0, seq_len, tm):
    m, d = x.shape
    fw, hw, cc = fox.shape[1], hg.shape[1], cv.shape[1]
    tn = _pick(d, 512, LANE)
    return pl.pallas_call(
        _outproj_kernel,
        grid=(m // tm, d // tn),
        in_specs=[pl.BlockSpec((tm, tn), lambda i, j: (i, j)),
                  _mod_spec(tm, seq_len, tn, (l, s, 2, r0), tiled=True),
                  pl.BlockSpec((tm, fw), lambda i, j: (i, 0)),
                  pl.BlockSpec((tm, hw), lambda i, j: (i, 0)),
                  pl.BlockSpec((tm, cc), lambda i, j: (i, 0)),
                  pl.BlockSpec((None, fw, tn), lambda i, j: (l, 0, j)),
                  pl.BlockSpec((None, hw, tn), lambda i, j: (l, fw // hw, j)),
                  pl.BlockSpec((None, cc, tn), lambda i, j: (l, (fw + hw) // cc, j))],
        out_specs=pl.BlockSpec((tm, tn), lambda i, j: (i, j)),
        out_shape=jax.ShapeDtypeStruct((m, d), F32),
        compiler_params=_cparams(("arbitrary", "arbitrary")),
        name="mix_outproj",
    )(x, mod6, fox, hg, cv, w_out, w_out, w_out)


def _norm_kernel(x_ref, g_ref, o_ref):
    x = x_ref[...]
    o_ref[...] = x * lax.rsqrt(jnp.mean(x * x, axis=-1, keepdims=True) + EPS) * g_ref[...]


def _final_norm(x, g, tm):
    m, d = x.shape
    return pl.pallas_call(
        _norm_kernel,
        grid=(m // tm,),
        in_specs=[pl.BlockSpec((tm, d), lambda i: (i, 0)), pl.BlockSpec((1, d), lambda i: (0, 0))],
        out_specs=pl.BlockSpec((tm, d), lambda i: (i, 0)),
        out_shape=jax.ShapeDtypeStruct((m, d), F32),
        compiler_params=_cparams(("arbitrary",)),
        name="final_norm",
    )(x, g)


def _time_major(a, b, t):
    h = a.shape[-1]
    return a.reshape(b, t, h).transpose(1, 0, 2).reshape(t, b * h)


def _batch_major(a, b):
    t = a.shape[0]
    return a.reshape(t, b, -1).transpose(1, 0, 2)


def kernel(x_prompt, x_sample, cache_fox_k, cache_fox_v, cache_fox_logf, state_hgrn, state_conv,
           c_prompt, c_sample, norm_g, ada_w, ada_b, ffn_w_in, ffn_w_out, mix_w_in, mix_w_out,
           fox_f_bias, hgrn_lb_logits, hgrn_norm_g, conv_dw_w, conv_dw_b, conv_ln_g, conv_ln_b,
           final_norm_g):
    bp, seq, d = x_prompt.shape
    bs, dseq, _ = x_sample.shape
    depth = norm_g.shape[0]
    nsub = norm_g.shape[1]
    past = cache_fox_k.shape[2]
    fh, hd = cache_fox_k.shape[3], cache_fox_k.shape[4]
    fw = fh * hd
    hh, hk = state_hgrn.shape[2], state_hgrn.shape[3]
    hw = hh * hk
    cc = conv_dw_b.shape[1]
    width = conv_dw_w.shape[1]
    qscale = float(hd) ** -0.5 * LOG2E

    lb_sm = jnp.cumsum(jax.nn.softmax(hgrn_lb_logits.astype(F32), axis=0), axis=0)
    lb_all = lb_sm - lb_sm[0:1]

    row0_sample = -(-bp // 16) * 16
    c_rows = jnp.concatenate([jnp.pad(c_prompt, ((0, row0_sample - bp), (0, 0))),
                              jnp.pad(c_sample, ((0, -bs % 16), (0, 0)))], axis=0)
    mod6 = _ada(jax.nn.silu(c_rows).astype(BF16), ada_w, ada_b).reshape(depth, -1, nsub, 3, 1, d)
    g3 = norm_g.reshape(depth * nsub, 1, d)

    tn_in = _pick(fw, 512, LANE)
    n_rest = mix_w_in.shape[2] - 3 * fw - fh
    assert n_rest % tn_in == 0 and fh <= tn_in
    w_rest = jnp.concatenate(
        [mix_w_in[:, :, 3 * fw + fh:], mix_w_in[:, :, 3 * fw:3 * fw + fh],
         jnp.zeros((depth, d, tn_in - fh), F32)], axis=2)
    fz0 = 3 * fw + n_rest
    blk_hgrn = 3 * fw // hw
    blk_conv = (3 * fw + 4 * hw) // cc
    ffn_bf16 = {}
    cache_kt = cache_fox_k.transpose(0, 1, 3, 4, 2)
    cache_vt = cache_fox_v.transpose(0, 1, 3, 4, 2)
    cgrp = min(CACHE_HEAD_GROUP, fh)

    def trunk(x, b, t, row0, cached):
        m = b * t
        tm = _pick(m, 1024, 8) if t >= 1024 else _pick(m, 512, t)
        tm_ffn = _pick(m, FFN_TM, 8) if t >= FFN_TM else tm
        tq = _pick(t, 1024, LANE)
        bias_row = lambda l: jnp.tile(fox_f_bias[l], b)[None]

        def ffn(x, l, s):
            if cached:
                x, *ffn_bf16[l, s] = _ffn(x, g3, mod6, ffn_w_in, ffn_w_in, ffn_w_out, l, s, row0, t,
                                          tm_ffn, True)
                return x
            return _ffn(x, g3, mod6, *ffn_bf16[l, s], l, s, row0, t, tm_ffn, False)[0]

        new = []
        for l in range(depth):
            x = ffn(x, l, 0)
            proj, qkv = _inproj(x, g3, mod6, mix_w_in, w_rest, l, 1, row0, fw, tn_in, qscale, t, tm)
            k_new = proj[:, fw:2 * fw].reshape(b, t, fh, hd)
            v_new = proj[:, 2 * fw:3 * fw].reshape(b, t, fh, hd)
            logf_tm, fcum = _logf_cumsum(_time_major(proj[:, fz0:fz0 + fh], b, t), bias_row(l), True)
            logf = _batch_major(logf_tm, b)
            if not cached:
                fq, fk = _pair_layouts(fcum * LOG2E, b, fh, hd)
                fox = _attn_prompt(qkv, qkv[:, 2 * fw:].T, fq, fk, b, t, fw, hd, tq, tq)
                s0 = jnp.zeros((b, hh, hk, hk), F32)
                buf = jnp.zeros((b, width - 1, cc), F32)
            else:
                cl_tm = cache_fox_logf[l].astype(F32).transpose(1, 0, 2).reshape(past, b * fh)
                _, fc = _logf_cumsum(cl_tm, jnp.zeros((1, b * fh), F32), False)
                f2_new = _batch_major((fcum + fc[-1:]) * LOG2E, b).reshape(b, t, fh // cgrp, cgrp)
                f2_old = _batch_major(fc * LOG2E, b).reshape(b, past, fh // cgrp, cgrp)
                fox = _attn_cached(qkv, cache_kt, cache_vt, f2_new.transpose(0, 2, 1, 3),
                                   f2_old.transpose(0, 2, 3, 1), f2_new.transpose(0, 2, 3, 1), l, b, t, fh, hd)
                s0 = state_hgrn[l]
                buf = state_conv[l]
            hg, s_fin = _hgrn(proj, lb_all[l][None], hgrn_norm_g[l][None], s0, b, t, blk_hgrn, hh, hk)
            cv, new_buf = _conv(proj, buf, conv_dw_w[l], conv_dw_b[l][None], conv_ln_g[l][None],
                                conv_ln_b[l][None], b, t, blk_conv, cc)
            x = _outproj(x, mod6, fox, hg, cv, mix_w_out, l, 1, row0, t, tm)
            x = ffn(x, l, 2)
            new.append((k_new, v_new, logf, s_fin, new_buf))
        y = _final_norm(x, final_norm_g[None], tm).reshape(b, t, d)
        return y, [jnp.stack([s[i] for s in new]) for i in range(5)]

    y_s, (ks, vs, fs, hs, cs) = trunk(x_sample.reshape(bs * dseq, d), bs, dseq, row0_sample, True)
    y_p, (kp, vp, fp, hp, cp) = trunk(x_prompt.reshape(bp * seq, d), bp, seq, 0, False)
    return (y_p, y_s, kp, vp, fp, hp, cp, ks, vs, fs, hs, cs)
```

```python
import functools

import jax
import jax.numpy as jnp
import numpy as np
from jax import lax
from jax.experimental import pallas as pl
from jax.experimental.pallas import tpu as pltpu

F32 = jnp.float32
BF16 = jnp.bfloat16
EPS = 1e-6
NEG = -1e30
LOG2E = 1.4426950408889634
LANE = 128
ONES_ROWS = 16
FROWS = 8
ATTN_QCHUNK = 512
HGRN_CHUNK = 64
CACHE_HEAD_GROUP = 4
HGRN_SUB = 16
FFN_TM, FFN_TF = 512, 512
FFN_TF_EMIT = 256
VMEM_LIMIT = 60 * 1024 * 1024

NT = (((1,), (1,)), ((), ()))
TN = (((0,), (0,)), ((), ()))


def _pick(n, target, mult=8):
    for t in range(min(n, target), 0, -1):
        if n % t == 0 and t % mult == 0:
            return t
    return n


def _cparams(sem):
    return pltpu.CompilerParams(dimension_semantics=sem, vmem_limit_bytes=VMEM_LIMIT)


def _sigmoid(x):
    return 1.0 / (1.0 + jnp.exp(-x))


def _split3(x):
    hi = x.astype(BF16)
    r1 = x - hi.astype(F32)
    mid = r1.astype(BF16)
    lo = (r1 - mid.astype(F32)).astype(BF16)
    return hi, mid, lo


def _lower_tri(n):
    return (lax.broadcasted_iota(jnp.int32, (n, n), 0)
            >= lax.broadcasted_iota(jnp.int32, (n, n), 1)).astype(BF16)


def _modulate(x, g, shift, scale):
    tm, d = x.shape
    grp = shift.shape[0]
    y = x * lax.rsqrt(jnp.mean(x * x, axis=-1, keepdims=True) + EPS) * g
    y = y.reshape(grp, tm // grp, d) * (1.0 + scale) + shift
    return y.reshape(tm, d)


def _gated_residual(x, gate, upd, coef):
    tm, d = x.shape
    grp = gate.shape[0]
    y = x.reshape(grp, tm // grp, d) + (coef * gate) * upd.reshape(grp, tm // grp, d)
    return y.reshape(tm, d)


def _mod_spec(tm, seq_len, tn, where, tiled=False):
    l, s, kind, r0 = where
    grp = max(1, tm // seq_len)
    assert r0 % grp == 0
    return pl.BlockSpec(
        (None, grp, None, None, 1, tn),
        lambda i, j: (l, r0 // grp + (i * tm) // (seq_len * grp), s, kind, 0, j if tiled else 0))


def _vec_spec(n, idx):
    return pl.BlockSpec((None, 1, n), lambda *_: (idx, 0, 0))


def _ada_kernel(c_ref, w_ref, b_ref, o_ref):
    w = w_ref[...].astype(BF16)
    o_ref[...] = jnp.dot(c_ref[...], w, preferred_element_type=F32) + b_ref[...]


def _ada(c_act, ada_w, ada_b):
    depth, d, n = ada_w.shape
    r = c_act.shape[0]
    tn = _pick(n, 1024, LANE)
    return pl.pallas_call(
        _ada_kernel,
        grid=(depth, n // tn),
        in_specs=[pl.BlockSpec((r, d), lambda l, j: (0, 0)),
                  pl.BlockSpec((None, d, tn), lambda l, j: (l, 0, j)),
                  pl.BlockSpec((None, 1, tn), lambda l, j: (l, 0, j))],
        out_specs=pl.BlockSpec((None, r, tn), lambda l, j: (l, 0, j)),
        out_shape=jax.ShapeDtypeStruct((depth, r, n), F32),
        compiler_params=_cparams(("arbitrary", "arbitrary")),
        name="ada_proj",
    )(c_act, ada_w, ada_b.reshape(depth, 1, n))


def _ffn_kernel(x_ref, g_ref, sh_ref, sc_ref, gt_ref, wa_ref, wb_ref, wo_ref, o_ref, *rest, nj, emit):
    if emit:
        wab_ref, wbb_ref, wob_ref, h_ref = rest
    else:
        (h_ref,) = rest
    j = pl.program_id(1)

    @pl.when(j == 0)
    def _():
        h_ref[...] = _modulate(x_ref[...], g_ref[...], sh_ref[...], sc_ref[...]).astype(BF16)
        o_ref[...] = jnp.zeros_like(o_ref)

    wa, wb, wo = wa_ref[...], wb_ref[...], wo_ref[...]
    if emit:
        wa, wb, wo = wa.astype(BF16), wb.astype(BF16), wo.astype(BF16)
        wab_ref[...] = wa
        wbb_ref[...] = wb
        wob_ref[...] = wo
    h = h_ref[...]
    a = jnp.dot(h, wa, preferred_element_type=F32)
    b = jnp.dot(h, wb, preferred_element_type=F32)
    act = (a * _sigmoid(a) * b).astype(BF16)
    o_ref[...] += jnp.dot(act, wo, preferred_element_type=F32)

    @pl.when(j == nj - 1)
    def _():
        o_ref[...] = _gated_residual(x_ref[...], gt_ref[...], o_ref[...], 0.5)


def _ffn(x, g3, mod6, wa, wb, wo, l, s, r0, seq_len, tm, emit):
    m, d = x.shape
    dff = wo.shape[-2]
    tf = _pick(dff, FFN_TF_EMIT if emit else FFN_TF, LANE)
    nj = dff // tf
    nsub = mod6.shape[2]
    mods = [_mod_spec(tm, seq_len, d, (l, s, kind, r0)) for kind in range(3)]
    if emit:
        assert m == tm
        w_specs = [pl.BlockSpec((None, None, d, tf), lambda i, j: (l, s // 2, 0, j)),
                   pl.BlockSpec((None, None, d, tf), lambda i, j: (l, s // 2, 0, j + nj)),
                   pl.BlockSpec((None, None, tf, d), lambda i, j: (l, s // 2, j, 0))]
    bf_specs = [pl.BlockSpec((d, tf), lambda i, j: (0, j)),
                pl.BlockSpec((d, tf), lambda i, j: (0, j)),
                pl.BlockSpec((tf, d), lambda i, j: (j, 0))]
    if not emit:
        w_specs = bf_specs
    out_specs = [pl.BlockSpec((tm, d), lambda i, j: (i, 0))]
    out_shape = [jax.ShapeDtypeStruct((m, d), F32)]
    if emit:
        out_specs += bf_specs
        out_shape += [jax.ShapeDtypeStruct((d, dff), BF16)] * 2 + [jax.ShapeDtypeStruct((dff, d), BF16)]
    return pl.pallas_call(
        functools.partial(_ffn_kernel, nj=nj, emit=emit),
        grid=(m // tm, nj),
        in_specs=[pl.BlockSpec((tm, d), lambda i, j: (i, 0)), _vec_spec(d, l * nsub + s)] + mods + w_specs,
        out_specs=out_specs,
        out_shape=out_shape,
        scratch_shapes=[pltpu.VMEM((tm, d), BF16)],
        compiler_params=_cparams(("arbitrary", "arbitrary")),
        name="ffn",
    )(x, g3, mod6, mod6, mod6, wa, wb, wo)


def _inproj_kernel(x_ref, g_ref, sh_ref, sc_ref, wq_ref, wr_ref, proj_ref, qkv_ref, h_ref,
                   *, nq, nqkv, qscale):
    j = pl.program_id(1)

    @pl.when(j == 0)
    def _():
        h_ref[...] = _modulate(x_ref[...], g_ref[...], sh_ref[...], sc_ref[...]).astype(BF16)

    @pl.when(j < nqkv)
    def _():
        r = lax.dot_general(h_ref[...], wq_ref[...].astype(BF16), NT, preferred_element_type=F32)
        proj_ref[...] = r
        qkv_ref[...] = (r * jnp.where(j < nq, qscale, 1.0)).astype(BF16)

    @pl.when(j >= nqkv)
    def _():
        proj_ref[...] = lax.dot_general(h_ref[...], wr_ref[...].astype(BF16), NT,
                                        preferred_element_type=F32)


def _inproj(x, g3, mod6, w_in, w_rest, l, s, r0, fw, tn, qscale, seq_len, tm):
    m, d = x.shape
    nq, nqkv = fw // tn, 3 * fw // tn
    n_pad = 3 * fw + w_rest.shape[1]
    nsub = mod6.shape[2]
    return pl.pallas_call(
        functools.partial(_inproj_kernel, nq=nq, nqkv=nqkv, qscale=qscale),
        grid=(m // tm, n_pad // tn),
        in_specs=[pl.BlockSpec((tm, d), lambda i, j: (i, 0)), _vec_spec(d, l * nsub + s),
                  _mod_spec(tm, seq_len, d, (l, s, 0, r0)), _mod_spec(tm, seq_len, d, (l, s, 1, r0)),
                  pl.BlockSpec((None, tn, d), lambda i, j: (l, jnp.minimum(j, nqkv - 1), 0)),
                  pl.BlockSpec((None, tn, d), lambda i, j: (l, jnp.maximum(j - nqkv, 0), 0))],
        out_specs=[pl.BlockSpec((tm, tn), lambda i, j: (i, j)),
                   pl.BlockSpec((tm, tn), lambda i, j: (i, jnp.minimum(j, nqkv - 1)))],
        out_shape=[jax.ShapeDtypeStruct((m, n_pad), F32),
                   jax.ShapeDtypeStruct((m, 3 * fw), BF16)],
        scratch_shapes=[pltpu.VMEM((tm, d), BF16)],
        compiler_params=_cparams(("arbitrary", "arbitrary")),
        name="mix_inproj",
    )(x, g3, mod6, mod6, w_in, w_rest)


def _cumsum_kernel(z_ref, bias_ref, lf_ref, f_ref, carry_ref, *, logsig):
    t = pl.program_id(0)

    @pl.when(t == 0)
    def _():
        carry_ref[...] = jnp.zeros_like(carry_ref)

    z = z_ref[...]
    if logsig:
        z = z + bias_ref[...]
        z = jnp.minimum(z, 0.0) - jnp.log(1.0 + jnp.exp(-jnp.abs(z)))
    lf_ref[...] = z
    tb = z.shape[0]
    tri = _lower_tri(tb)
    hi, mid, lo = _split3(z)
    cs = (jnp.dot(tri, hi, preferred_element_type=F32) + jnp.dot(tri, mid, preferred_element_type=F32)
          + jnp.dot(tri, lo, preferred_element_type=F32))
    f = cs + carry_ref[...]
    f_ref[...] = f
    carry_ref[...] = f[tb - 1:tb, :]


def _logf_cumsum(z, bias, logsig):
    t, n = z.shape
    tb = _pick(t, 512, 8)
    spec = pl.BlockSpec((tb, n), lambda i: (i, 0))
    return pl.pallas_call(
        functools.partial(_cumsum_kernel, logsig=logsig),
        grid=(t // tb,),
        in_specs=[spec, pl.BlockSpec((1, n), lambda i: (0, 0))],
        out_specs=[spec, spec],
        out_shape=[jax.ShapeDtypeStruct((t, n), F32)] * 2,
        scratch_shapes=[pltpu.VMEM((1, n), F32)],
        compiler_params=_cparams(("arbitrary",)),
        name="logf_cumsum",
    )(z, bias)


def _attn_kernel(qt_ref, kt_ref, q_ref, k_ref, vt_ref, fq_ref, fk_ref, o_ref,
                 qm_ref, m_ref, acc_ref, *, tq, tk, hd):
    t = pl.program_id(2)
    qi, ki = qt_ref[t], kt_ref[t]
    hpp = LANE // hd
    ones = jnp.ones((ONES_ROWS, tk), BF16)

    @pl.when(ki == 0)
    def _():
        lane = lax.broadcasted_iota(jnp.int32, (1, LANE), 1)
        q = q_ref[...]
        for h in range(hpp):
            qm_ref[h] = jnp.where((lane >= hd * h) & (lane < hd * (h + 1)), q, 0)
        m_ref[...] = jnp.full_like(m_ref, NEG)
        acc_ref[...] = jnp.zeros_like(acc_ref)

    def body(masked):
        kk = k_ref[...]
        cw = min(ATTN_QCHUNK, tq)
        if masked:
            rel = (lax.broadcasted_iota(jnp.int32, (tk, cw), 0)
                   - lax.broadcasted_iota(jnp.int32, (tk, cw), 1))
        chains = [(h, slice(c0, c0 + cw), c0) for c0 in range(0, tq, cw) for h in range(hpp)]
        state = [(m_ref[h, :, cols], acc_ref[h, :, cols]) for h, cols, _ in chains]
        new_state = []
        scores = [lax.dot_general(kk, qm_ref[h, cols, :], NT, preferred_element_type=F32)
                  for h, cols, _ in chains]
        fk_cols = fk_ref[...].T
        for (h, cols, c0), (m_prev, acc_prev), s in zip(chains, state, scores):
            u = s - fk_cols[:, h:h + 1]
            if masked:
                u = jnp.where(rel <= qi * tq + c0 - ki * tk, u, NEG)
            fq = fq_ref[h:h + 1, cols]
            m_new = jnp.maximum(m_prev, jnp.max(u, axis=0, keepdims=True) + fq)
            alpha = jnp.exp2(m_prev - m_new)
            p = jnp.exp2(u + (fq - m_new)).astype(BF16)
            v_one = jnp.concatenate([vt_ref[h * hd:(h + 1) * hd, :], ones], axis=0)
            new_state.append((m_new, alpha * acc_prev + jnp.dot(v_one, p, preferred_element_type=F32)))
        for (h, cols, _), (m_new, acc_new) in zip(chains, new_state):
            m_ref[h, :, cols] = m_new
            acc_ref[h, :, cols] = acc_new

    needs_mask = (ki + 1) * tk - 1 > qi * tq

    @pl.when(needs_mask)
    def _():
        body(True)

    @pl.when(jnp.logical_not(needs_mask))
    def _():
        body(False)

    @pl.when((ki + 1) * tk >= (qi + 1) * tq)
    def _():
        o = jnp.concatenate([acc_ref[h, 0:hd] * (1.0 / acc_ref[h, hd:hd + 1]) for h in range(hpp)],
                            axis=0)
        o_ref[...] = o.T.astype(BF16)


def _attn_prompt(qkv, vt, f2rows, b, t, fw, hd, tq, tk):
    npair = fw // LANE
    nq, nkb = t // tq, t // tk
    pairs = [(i, j) for i in range(nq) for j in range(((i + 1) * tq - 1) // tk + 1)]
    qt = jnp.asarray(np.array([p[0] for p in pairs], np.int32))
    kt = jnp.asarray(np.array([p[1] for p in pairs], np.int32))
    hpp = LANE // hd
    grid_spec = pltpu.PrefetchScalarGridSpec(
        num_scalar_prefetch=2,
        grid=(b, npair, len(pairs)),
        in_specs=[pl.BlockSpec((tq, LANE), lambda bi, p, s, qt, kt: (bi * nq + qt[s], p)),
                  pl.BlockSpec((tk, LANE), lambda bi, p, s, qt, kt: (bi * nkb + kt[s], npair + p)),
                  pl.BlockSpec((LANE, tk), lambda bi, p, s, qt, kt: (p, bi * nkb + kt[s])),
                  pl.BlockSpec((None, None, FROWS, tq), lambda bi, p, s, qt, kt: (bi, p, 0, qt[s])),
                  pl.BlockSpec((None, None, FROWS, tk), lambda bi, p, s, qt, kt: (bi, p, 0, kt[s]))],
        out_specs=pl.BlockSpec((tq, LANE), lambda bi, p, s, qt, kt: (bi * nq + qt[s], p)),
        scratch_shapes=[pltpu.VMEM((hpp, tq, LANE), BF16),
                        pltpu.VMEM((hpp, 1, tq), F32),
                        pltpu.VMEM((hpp, hd + ONES_ROWS, tq), F32)],
    )
    return pl.pallas_call(
        functools.partial(_attn_kernel, tq=tq, tk=tk, hd=hd),
        grid_spec=grid_spec,
        out_shape=jax.ShapeDtypeStruct((b * t, fw), BF16),
        compiler_params=_cparams(("arbitrary", "arbitrary", "arbitrary")),
        name="fox_attn_prompt",
    )(qt, kt, qkv, qkv, vt, f2rows, f2rows)


def _pair_layouts(f2, b, fh, hd):
    t = f2.shape[0]
    hpp = LANE // hd
    f4 = f2.reshape(t, b, fh // hpp, hpp).transpose(1, 2, 3, 0)
    return jnp.pad(f4, ((0, 0), (0, 0), (0, FROWS - hpp), (0, 0)))


def _attn_cache_kernel(q_ref, kn_ref, vn_ref, kt_ref, vt_ref, fq_ref, fkc_ref, fkn_ref, o_ref, *, g, hd):
    s_len = q_ref.shape[0]
    causal = (lax.broadcasted_iota(jnp.int32, (s_len, s_len), 1)
              <= lax.broadcasted_iota(jnp.int32, (s_len, s_len), 0))
    heads = range(g)
    cols = [slice(i * hd, (i + 1) * hd) for i in heads]
    q = [q_ref[:, c] for c in cols]
    s1 = [jnp.dot(q[i], kt_ref[i].astype(BF16), preferred_element_type=F32) for i in heads]
    s2 = [lax.dot_general(q[i], kn_ref[:, cols[i]], NT, preferred_element_type=F32) for i in heads]
    p1, p2, den = [], [], []
    for i in heads:
        fq = fq_ref[:, i:i + 1]
        a1 = s1[i] + (fq - fkc_ref[i:i + 1, :])
        a2 = jnp.where(causal, s2[i] + (fq - fkn_ref[i:i + 1, :]), NEG)
        m = jnp.maximum(jnp.max(a1, axis=-1, keepdims=True), jnp.max(a2, axis=-1, keepdims=True))
        e1 = jnp.exp2(a1 - m)
        e2 = jnp.exp2(a2 - m)
        den.append(jnp.sum(e1, axis=-1, keepdims=True) + jnp.sum(e2, axis=-1, keepdims=True))
        p1.append(e1.astype(BF16))
        p2.append(e2.astype(BF16))
    pv = [lax.dot_general(p1[i], vt_ref[i].astype(BF16), NT, preferred_element_type=F32)
          + jnp.dot(p2[i], vn_ref[:, cols[i]], preferred_element_type=F32) for i in heads]
    o_ref[...] = jnp.concatenate([pv[i] * (1.0 / den[i]) for i in heads], axis=1).astype(BF16)


def _attn_cached(qkv, cache_kt, cache_vt, fq, fkc, fkn, l, b, s_len, fh, hd):
    fw = fh * hd
    past = cache_kt.shape[4]
    g = min(CACHE_HEAD_GROUP, fh)
    ng = fh // g
    cspec = pl.BlockSpec((None, None, g, hd, past), lambda bi, gi: (l, bi, gi, 0, 0))
    return pl.pallas_call(
        functools.partial(_attn_cache_kernel, g=g, hd=hd),
        grid=(b, ng),
        in_specs=[pl.BlockSpec((s_len, g * hd), lambda bi, gi: (bi, gi)),
                  pl.BlockSpec((s_len, g * hd), lambda bi, gi: (bi, ng + gi)),
                  pl.BlockSpec((s_len, g * hd), lambda bi, gi: (bi, 2 * ng + gi)),
                  cspec, cspec,
                  pl.BlockSpec((None, None, s_len, g), lambda bi, gi: (bi, gi, 0, 0)),
                  pl.BlockSpec((None, None, g, past), lambda bi, gi: (bi, gi, 0, 0)),
                  pl.BlockSpec((None, None, g, s_len), lambda bi, gi: (bi, gi, 0, 0))],
        out_specs=pl.BlockSpec((s_len, g * hd), lambda bi, gi: (bi, gi)),
        out_shape=jax.ShapeDtypeStruct((b * s_len, fw), BF16),
        compiler_params=_cparams(("arbitrary", "arbitrary")),
        name="fox_attn_cached",
    )(qkv, qkv, qkv, cache_kt, cache_vt, fq, fkc, fkn)


def _hgrn_kernel(q_ref, f_ref, i_ref, g_ref, lb_ref, gn_ref, s0_ref, o_ref, sfin_ref, st_ref,
                 *, c, nc, hh, hk):
    ci = pl.program_id(1)

    @pl.when(ci == 0)
    def _():
        for h in range(hh):
            st_ref[h] = s0_ref[h].astype(F32).T

    lb = lb_ref[...]
    f = lb + (1.0 - lb) * _sigmoid(f_ref[...])
    k_all = 1.0 - f
    g = jnp.log(f)
    cl = g.shape[0]
    hw = hh * hk

    b3 = jnp.dot(_lower_tri(cl), jnp.concatenate(_split3(g), axis=1), preferred_element_type=F32)
    b_all = b3[:, :hw] + b3[:, hw:2 * hw] + b3[:, 2 * hw:]
    q_all, v_all, gate_all = q_ref[...], i_ref[...], g_ref[...]

    sub_s = lax.broadcasted_iota(jnp.int32, (c, c, c), 0)
    sub_t = lax.broadcasted_iota(jnp.int32, (c, c, c), 1)
    sub_n = lax.broadcasted_iota(jnp.int32, (c, c, c), 2)
    pick = (sub_n == sub_s) & (sub_t >= sub_s)
    heads = range(hh)
    sls = [slice(h * hk, (h + 1) * hk) for h in heads]
    q = [q_all[:, sl] for sl in sls]
    k = [k_all[:, sl] for sl in sls]
    b = [b_all[:, sl] for sl in sls]
    kb = [x.astype(BF16) for x in k]
    vb = [v_all[:, sl].astype(BF16) for sl in sls]
    st = [st_ref[h] for h in heads]
    b_last = [x[cl - 1:cl] for x in b]

    k_end = [(k[h] * jnp.exp(b_last[h] - b[h])).astype(BF16) for h in heads]
    upd = [lax.dot_general(vb[h], k_end[h], TN, preferred_element_type=F32) for h in heads]
    o_inter = [lax.dot_general((q[h] * jnp.exp(b[h])).astype(BF16), st[h].astype(BF16), NT,
                               preferred_element_type=F32) for h in heads]
    for h in heads:
        st_ref[h] = st[h] * jnp.exp(b_last[h]) + upd[h]

    outs = [[] for _ in heads]
    for i in range(cl // c):
        lo_r, hi_r = i * c, (i + 1) * c
        qi = [x[lo_r:hi_r] for x in q]
        bi = [x[lo_r:hi_r] for x in b]
        x = [jnp.concatenate(
            [qi[h] * jnp.exp(jnp.minimum(bi[h] - b[h][lo_r + s:lo_r + s + 1], 0.0)) for s in range(c)],
            axis=0).astype(BF16) for h in heads]
        r = [lax.dot_general(x[h], kb[h][lo_r:hi_r], NT, preferred_element_type=F32) for h in heads]
        a_diag = [jnp.sum(jnp.where(pick, r[h].reshape(c, c, c), 0.0), axis=0).astype(BF16)
                  for h in heads]
        oi = [jnp.dot(a_diag[h], vb[h][lo_r:hi_r], preferred_element_type=F32) for h in heads]
        if i > 0:
            ref = [x_[lo_r - 1:lo_r] for x_ in b]
            qd = [(qi[h] * jnp.exp(bi[h] - ref[h])).astype(BF16) for h in heads]
            kdec = [(k[h][:lo_r] * jnp.exp(ref[h] - b[h][:lo_r])).astype(BF16) for h in heads]
            a_off = [lax.dot_general(qd[h], kdec[h], NT, preferred_element_type=F32).astype(BF16)
                     for h in heads]
            oi = [oi[h] + jnp.dot(a_off[h], vb[h][:lo_r], preferred_element_type=F32) for h in heads]
        for h in heads:
            outs[h].append(oi[h])

    ys = []
    for h in heads:
        o = o_inter[h] + jnp.concatenate(outs[h], axis=0)
        y = o * lax.rsqrt(jnp.mean(o * o, axis=-1, keepdims=True) + EPS) * gn_ref[...]
        gate = gate_all[:, sls[h]]
        ys.append(y * (gate * _sigmoid(gate)))
    o_ref[...] = jnp.concatenate(ys, axis=1).astype(BF16)

    @pl.when(ci == nc - 1)
    def _():
        for h in range(hh):
            sfin_ref[h] = st_ref[h].T


def _hgrn(proj, lb, gn, s0, b, t, col0, hh, hk):
    cl = _pick(t, HGRN_CHUNK, HGRN_SUB)
    nc = t // cl
    hw = hh * hk

    def col(kind):
        return pl.BlockSpec((cl, hw), lambda bi, ci: (bi * nc + ci, col0 + kind))

    return pl.pallas_call(
        functools.partial(_hgrn_kernel, c=HGRN_SUB, nc=nc, hh=hh, hk=hk),
        grid=(b, nc),
        in_specs=[col(0), col(1), col(2), col(3),
                  pl.BlockSpec((1, hw), lambda bi, ci: (0, 0)),
                  pl.BlockSpec((1, hk), lambda bi, ci: (0, 0)),
                  pl.BlockSpec((None, hh, hk, hk), lambda bi, ci: (bi, 0, 0, 0))],
        out_specs=[pl.BlockSpec((cl, hw), lambda bi, ci: (bi * nc + ci, 0)),
                   pl.BlockSpec((None, hh, hk, hk), lambda bi, ci: (bi, 0, 0, 0))],
        out_shape=[jax.ShapeDtypeStruct((b * t, hw), BF16),
                   jax.ShapeDtypeStruct((b, hh, hk, hk), F32)],
        scratch_shapes=[pltpu.VMEM((hh, hk, hk), F32)],
        compiler_params=_cparams(("arbitrary", "arbitrary")),
        name="hgrn2",
    )(proj, proj, proj, proj, lb, gn, s0)


def _conv_kernel(ga_ref, gb_ref, buf_ref, w_ref, b_ref, lg_ref, lb_ref, o_ref, nb_ref, xs_ref,
                 *, tb, nblk, width, halo, rb):
    t = pl.program_id(1)
    pad = halo - (width - 1)

    @pl.when(t == 0)
    def _():
        xs_ref[0:halo, :] = jnp.zeros((halo, xs_ref.shape[1]), F32)
        xs_ref[pad:halo, :] = buf_ref[...].astype(F32)

    @pl.when(t > 0)
    def _():
        xs_ref[0:halo, :] = xs_ref[tb:tb + halo, :]

    xs_ref[halo:halo + tb, :] = ga_ref[...] * _sigmoid(gb_ref[...])
    for r0 in range(0, tb, rb):
        acc = xs_ref[r0 + pad:r0 + pad + rb, :] * w_ref[0:1, :]
        for w in range(1, width):
            acc = acc + xs_ref[r0 + pad + w:r0 + pad + w + rb, :] * w_ref[w:w + 1, :]
        y = acc + b_ref[...]
        mu = jnp.mean(y, axis=-1, keepdims=True)
        yc = y - mu
        var = jnp.mean(yc * yc, axis=-1, keepdims=True)
        z = yc * lax.rsqrt(var + EPS) * lg_ref[...] + lb_ref[...]
        o_ref[r0:r0 + rb, :] = (z * _sigmoid(z)).astype(BF16)

    @pl.when(t == nblk - 1)
    def _():
        nb_ref[...] = xs_ref[tb + pad:tb + halo, :]


def _conv(proj, buf, dw_w, dw_b, ln_g, ln_b, b, t, ga_blk, cc):
    width = dw_w.shape[0]
    halo = -(-(width - 1) // 8) * 8
    tb = _pick(t, 256, 8)
    nblk = t // tb
    rb = _pick(tb, 64, 8)
    vec = pl.BlockSpec((1, cc), lambda bi, j: (0, 0))
    return pl.pallas_call(
        functools.partial(_conv_kernel, tb=tb, nblk=nblk, width=width, halo=halo, rb=rb),
        grid=(b, nblk),
        in_specs=[pl.BlockSpec((tb, cc), lambda bi, j: (bi * nblk + j, ga_blk)),
                  pl.BlockSpec((tb, cc), lambda bi, j: (bi * nblk + j, ga_blk + 1)),
                  pl.BlockSpec((None, width - 1, cc), lambda bi, j: (bi, 0, 0)),
                  pl.BlockSpec((width, cc), lambda bi, j: (0, 0)),
                  vec, vec, vec],
        out_specs=[pl.BlockSpec((tb, cc), lambda bi, j: (bi * nblk + j, 0)),
                   pl.BlockSpec((None, width - 1, cc), lambda bi, j: (bi, 0, 0))],
        out_shape=[jax.ShapeDtypeStruct((b * t, cc), BF16),
                   jax.ShapeDtypeStruct((b, width - 1, cc), F32)],
        scratch_shapes=[pltpu.VMEM((halo + tb, cc), F32)],
        compiler_params=_cparams(("arbitrary", "arbitrary")),
        name="conv_module",
    )(proj, proj, buf, dw_w, dw_b, ln_g, ln_b)


def _outproj_kernel(x_ref, gt_ref, fox_ref, hg_ref, cv_ref, w1_ref, w2_ref, w3_ref, o_ref):
    acc = (jnp.dot(fox_ref[...], w1_ref[...].astype(BF16), preferred_element_type=F32)
           + jnp.dot(hg_ref[...], w2_ref[...].astype(BF16), preferred_element_type=F32)
           + jnp.dot(cv_ref[...], w3_ref[...].astype(BF16), preferred_element_type=F32))
    o_ref[...] = _gated_residual(x_ref[...], gt_ref[...], acc, 1.0)


def _outproj(x, mod6, fox, hg, cv, w_out, l, s, r0, seq_len, tm):
    m, d = x.shape
    fw, hw, cc = fox.shape[1], hg.shape[1], cv.shape[1]
    tn = _pick(d, 512, LANE)
    return pl.pallas_call(
        _outproj_kernel,
        grid=(m // tm, d // tn),
        in_specs=[pl.BlockSpec((tm, tn), lambda i, j: (i, j)),
                  _mod_spec(tm, seq_len, tn, (l, s, 2, r0), tiled=True),
                  pl.BlockSpec((tm, fw), lambda i, j: (i, 0)),
                  pl.BlockSpec((tm, hw), lambda i, j: (i, 0)),
                  pl.BlockSpec((tm, cc), lambda i, j: (i, 0)),
                  pl.BlockSpec((None, fw, tn), lambda i, j: (l, 0, j)),
                  pl.BlockSpec((None, hw, tn), lambda i, j: (l, fw // hw, j)),
                  pl.BlockSpec((None, cc, tn), lambda i, j: (l, (fw + hw) // cc, j))],
        out_specs=pl.BlockSpec((tm, tn), lambda i, j: (i, j)),
        out_shape=jax.ShapeDtypeStruct((m, d), F32),
        compiler_params=_cparams(("arbitrary", "arbitrary")),
        name="mix_outproj",
    )(x, mod6, fox, hg, cv, w_out, w_out, w_out)


def _norm_kernel(x_ref, g_ref, o_ref):
    x = x_ref[...]
    o_ref[...] = x * lax.rsqrt(jnp.mean(x * x, axis=-1, keepdims=True) + EPS) * g_ref[...]


def _final_norm(x, g, tm):
    m, d = x.shape
    return pl.pallas_call(
        _norm_kernel,
        grid=(m // tm,),
        in_specs=[pl.BlockSpec((tm, d), lambda i: (i, 0)), pl.BlockSpec((1, d), lambda i: (0, 0))],
        out_specs=pl.BlockSpec((tm, d), lambda i: (i, 0)),
        out_shape=jax.ShapeDtypeStruct((m, d), F32),
        compiler_params=_cparams(("arbitrary",)),
        name="final_norm",
    )(x, g)


def _time_major(a, b, t):
    h = a.shape[-1]
    return a.reshape(b, t, h).transpose(1, 0, 2).reshape(t, b * h)


def _batch_major(a, b):
    t = a.shape[0]
    return a.reshape(t, b, -1).transpose(1, 0, 2)


def kernel(x_prompt, x_sample, cache_fox_k, cache_fox_v, cache_fox_logf, state_hgrn, state_conv,
           c_prompt, c_sample, norm_g, ada_w, ada_b, ffn_w_in, ffn_w_out, mix_w_in, mix_w_out,
           fox_f_bias, hgrn_lb_logits, hgrn_norm_g, conv_dw_w, conv_dw_b, conv_ln_g, conv_ln_b,
           final_norm_g):
    bp, seq, d = x_prompt.shape
    bs, dseq, _ = x_sample.shape
    depth = norm_g.shape[0]
    nsub = norm_g.shape[1]
    past = cache_fox_k.shape[2]
    fh, hd = cache_fox_k.shape[3], cache_fox_k.shape[4]
    fw = fh * hd
    hh, hk = state_hgrn.shape[2], state_hgrn.shape[3]
    hw = hh * hk
    cc = conv_dw_b.shape[1]
    width = conv_dw_w.shape[1]
    qscale = float(hd) ** -0.5 * LOG2E

    lb_sm = jnp.cumsum(jax.nn.softmax(hgrn_lb_logits.astype(F32), axis=0), axis=0)
    lb_all = lb_sm - lb_sm[0:1]

    row0_sample = -(-bp // 16) * 16
    c_rows = jnp.concatenate([jnp.pad(c_prompt, ((0, row0_sample - bp), (0, 0))),
                              jnp.pad(c_sample, ((0, -bs % 16), (0, 0)))], axis=0)
    mod6 = _ada(jax.nn.silu(c_rows).astype(BF16), ada_w, ada_b).reshape(depth, -1, nsub, 3, 1, d)
    g3 = norm_g.reshape(depth * nsub, 1, d)

    tn_in = _pick(fw, 512, LANE)
    n_rest = mix_w_in.shape[2] - 3 * fw - fh
    assert n_rest % tn_in == 0 and fh <= tn_in
    mix_wt = mix_w_in.transpose(0, 2, 1)
    w_rest = jnp.concatenate(
        [mix_wt[:, 3 * fw + fh:], mix_wt[:, 3 * fw:3 * fw + fh],
         jnp.zeros((depth, tn_in - fh, d), F32)], axis=1)
    fz0 = 3 * fw + n_rest
    blk_hgrn = 3 * fw // hw
    blk_conv = (3 * fw + 4 * hw) // cc
    ffn_bf16 = {}
    cache_kt = cache_fox_k.transpose(0, 1, 3, 4, 2)
    cache_vt = cache_fox_v.transpose(0, 1, 3, 4, 2)
    cgrp = min(CACHE_HEAD_GROUP, fh)

    def trunk(x, b, t, row0, cached):
        m = b * t
        tm = _pick(m, 1024, 8) if t >= 1024 else _pick(m, 512, t)
        tm_ffn = _pick(m, FFN_TM, 8) if t >= FFN_TM else tm
        tq = _pick(t, 1024, LANE)
        bias_row = lambda l: jnp.tile(fox_f_bias[l], b)[None]

        def ffn(x, l, s):
            if cached:
                x, *ffn_bf16[l, s] = _ffn(x, g3, mod6, ffn_w_in, ffn_w_in, ffn_w_out, l, s, row0, t,
                                          tm_ffn, True)
                return x
            return _ffn(x, g3, mod6, *ffn_bf16[l, s], l, s, row0, t, tm_ffn, False)[0]

        new = []
        for l in range(depth):
            x = ffn(x, l, 0)
            proj, qkv = _inproj(x, g3, mod6, mix_wt, w_rest, l, 1, row0, fw, tn_in, qscale, t, tm)
            k_new = proj[:, fw:2 * fw].reshape(b, t, fh, hd)
            v_new = proj[:, 2 * fw:3 * fw].reshape(b, t, fh, hd)
            logf_tm, fcum = _logf_cumsum(_time_major(proj[:, fz0:fz0 + fh], b, t), bias_row(l), True)
            logf = _batch_major(logf_tm, b)
            if not cached:
                f2rows = _pair_layouts(fcum * LOG2E, b, fh, hd)
                fox = _attn_prompt(qkv, qkv[:, 2 * fw:].T, f2rows, b, t, fw, hd, tq, tq)
                s0 = jnp.zeros((b, hh, hk, hk), F32)
                buf = jnp.zeros((b, width - 1, cc), F32)
            else:
                cl_tm = cache_fox_logf[l].astype(F32).transpose(1, 0, 2).reshape(past, b * fh)
                _, fc = _logf_cumsum(cl_tm, jnp.zeros((1, b * fh), F32), False)
                f2_new = _batch_major((fcum + fc[-1:]) * LOG2E, b).reshape(b, t, fh // cgrp, cgrp)
                f2_old = _batch_major(fc * LOG2E, b).reshape(b, past, fh // cgrp, cgrp)
                fox = _attn_cached(qkv, cache_kt, cache_vt, f2_new.transpose(0, 2, 1, 3),
                                   f2_old.transpose(0, 2, 3, 1), f2_new.transpose(0, 2, 3, 1), l, b, t, fh, hd)
                s0 = state_hgrn[l]
                buf = state_conv[l]
            hg, s_fin = _hgrn(proj, lb_all[l][None], hgrn_norm_g[l][None], s0, b, t, blk_hgrn, hh, hk)
            cv, new_buf = _conv(proj, buf, conv_dw_w[l], conv_dw_b[l][None], conv_ln_g[l][None],
                                conv_ln_b[l][None], b, t, blk_conv, cc)
            x = _outproj(x, mod6, fox, hg, cv, mix_w_out, l, 1, row0, t, tm)
            x = ffn(x, l, 2)
            new.append((k_new, v_new, logf, s_fin, new_buf))
        y = _final_norm(x, final_norm_g[None], tm).reshape(b, t, d)
        return y, [jnp.stack([s[i] for s in new]) for i in range(5)]

    y_s, (ks, vs, fs, hs, cs) = trunk(x_sample.reshape(bs * dseq, d), bs, dseq, row0_sample, True)
    y_p, (kp, vp, fp, hp, cp) = trunk(x_prompt.reshape(bp * seq, d), bp, seq, 0, False)
    return (y_p, y_s, kp, vp, fp, hp, cp, ks, vs, fs, hs, cs)
```
